```python
import jax, jax.numpy as jnp
from jax import lax
import numpy as np

D_MODEL = 2048
BATCH = 1
SEQ = 8192
DEPTH = 2
DEC_BATCH = 128
DEC_SEQ = 4
PAST_LEN = 16384
PAGE_SIZE = 128

MLA_HEADS = 8
MLA_NOPE = 128
MLA_ROPE = 64
MLA_QK = MLA_NOPE + MLA_ROPE
MLA_V = 128
MLA_Q_LORA = 512
MLA_KV_LORA = 512
ROPE_THETA = 10000.0
MOBA_HEADS = 8
MOBA_KV_HEADS = 2
MOBA_REP = MOBA_HEADS // MOBA_KV_HEADS
MOBA_DIM = 128
MOBA_BLOCK = 256
MOBA_TOPK = 3
CONV_DIM = D_MODEL
CONV_WIDTH = 3
WIDTH_A = MLA_HEADS * MLA_V
WIDTH_B = MOBA_HEADS * MOBA_DIM
EVEN_SPLITS = (MLA_Q_LORA, MLA_KV_LORA + MLA_ROPE, MOBA_HEADS * MOBA_DIM, MOBA_KV_HEADS * MOBA_DIM,
               MOBA_KV_HEADS * MOBA_DIM, WIDTH_A, WIDTH_B)
EVEN_IN = sum(EVEN_SPLITS)
ODD_IN = 4 * CONV_DIM
N_EVEN = (DEPTH + 1) // 2
N_ODD = DEPTH // 2
Q_BLOCK = 128
EPS = 1e-6

kernel_name = 'mla_moba_shortconv_hybrid_step'


def rmsnorm(x, g):
    xf = x.astype(jnp.float32)
    r = lax.rsqrt(jnp.mean(xf * xf, axis=-1, keepdims=True) + EPS)
    return (xf * r).astype(x.dtype) * g


def mla_gain(g):
    return jnp.concatenate([g[:MLA_NOPE], g[MLA_NOPE:], g[MLA_NOPE:]])


def rope(x, pos):
    half = MLA_ROPE // 2
    inv = ROPE_THETA ** (-jnp.arange(half, dtype=jnp.float32) / half)
    ang = pos.astype(jnp.float32)[:, None] * inv
    ang = ang.reshape(ang.shape[:1] + (1,) * (x.ndim - 3) + (half,))
    cos, sin = jnp.cos(ang), jnp.sin(ang)
    x1 = x[..., :half].astype(jnp.float32)
    x2 = x[..., half:].astype(jnp.float32)
    return jnp.concatenate([x1 * cos - x2 * sin, x2 * cos + x1 * sin], axis=-1).astype(x.dtype)


def mla_keys(c, kpe, w_uk, g):
    k_nope = jnp.einsum('...c,chd->...hd', c, w_uk.reshape(MLA_KV_LORA, MLA_HEADS, MLA_NOPE))
    k_pe = jnp.broadcast_to(kpe[..., None, :], k_nope.shape[:-1] + (MLA_ROPE,))
    return rmsnorm(jnp.concatenate([k_nope, k_pe], axis=-1), mla_gain(g))


def even_project(x, pos, norm, w_in, qa_norm, w_uq, kva_norm, q_norm, moba_qn, moba_kn):
    B, T = x.shape[:2]
    offs = tuple(int(o) for o in np.cumsum(EVEN_SPLITS)[:-1])
    z = jnp.einsum('btd,de->bte', rmsnorm(x, norm), w_in)
    q_lat, kv_a, q_b, k_b, v_b, g_a, g_b = jnp.split(z, offs, axis=-1)
    q = jnp.einsum('btr,re->bte', rmsnorm(q_lat, qa_norm), w_uq).reshape(B, T, MLA_HEADS, MLA_QK)
    q = jnp.concatenate([q[..., :MLA_NOPE], rope(q[..., MLA_NOPE:], pos)], axis=-1)
    q_a = rmsnorm(q, mla_gain(q_norm))
    c = rmsnorm(kv_a[..., :MLA_KV_LORA], kva_norm)
    kpe = rope(kv_a[..., MLA_KV_LORA:], pos)
    q_b = rmsnorm(q_b.reshape(B, T, MOBA_HEADS, MOBA_DIM), moba_qn)
    k_b = rmsnorm(k_b.reshape(B, T, MOBA_KV_HEADS, MOBA_DIM), moba_kn)
    v_b = v_b.reshape(B, T, MOBA_KV_HEADS, MOBA_DIM)
    return q_a, c, kpe, q_b, k_b, v_b, g_a, g_b


def even_merge(x, oa, ob, g_a, g_b, w_out):
    B, T = x.shape[:2]
    mix = jnp.concatenate([oa.reshape(B, T, WIDTH_A) * jax.nn.silu(g_a),
                           ob.reshape(B, T, WIDTH_B) * jax.nn.silu(g_b)], axis=-1)
    return x + jnp.einsum('bte,ed->btd', mix, w_out)


def mla_prompt_attn(q, k, v):
    B, S = q.shape[:2]
    nq = S // Q_BLOCK
    qb = q.reshape(B, nq, Q_BLOCK, MLA_HEADS, MLA_QK).swapaxes(0, 1)
    k_pos = jnp.arange(S)
    scale = MLA_QK ** -0.5

    def step(a):
        qblk, p0 = a
        s = jnp.einsum('bqhd,bkhd->bhqk', qblk, k, preferred_element_type=jnp.float32) * scale
        mask = k_pos[None, :] <= (p0 + jnp.arange(Q_BLOCK))[:, None]
        p = jax.nn.softmax(jnp.where(mask, s, -jnp.inf), axis=-1).astype(v.dtype)
        return jnp.einsum('bhqk,bkhd->bqhd', p, v)

    o = lax.map(step, (qb, jnp.arange(nq) * Q_BLOCK))
    return o.swapaxes(0, 1).reshape(B, S, MLA_HEADS, MLA_V)


def moba_blocks(k, v):
    B, T = k.shape[:2]
    nb = -(-T // MOBA_BLOCK)
    padw = ((0, 0), (0, nb * MOBA_BLOCK - T), (0, 0), (0, 0))
    kb = jnp.pad(k, padw).reshape(B, nb, MOBA_BLOCK, MOBA_KV_HEADS, MOBA_DIM)
    vb = jnp.pad(v, padw).reshape(B, nb, MOBA_BLOCK, MOBA_KV_HEADS, MOBA_DIM)
    kmean = jnp.mean(kb, axis=2, dtype=jnp.float32)
    return kb.transpose(0, 3, 1, 2, 4), vb.transpose(0, 3, 1, 2, 4), kmean


def moba_attend(q, q_pos, kbt, vbt, kmean):
    B, Tq = q.shape[:2]
    nb = kbt.shape[2]
    qg = q.reshape(B, Tq, MOBA_KV_HEADS, MOBA_REP, MOBA_DIM)
    gate = jnp.einsum('bqgrd,bngd->bqgrn', qg.astype(jnp.float32), kmean).reshape(B, Tq, MOBA_HEADS, nb)
    cur = q_pos // MOBA_BLOCK
    past_ok = jnp.arange(nb)[None, :] < cur[:, None]
    gate = jnp.where(past_ok[None, :, None, :], gate, -jnp.inf)
    if nb < MOBA_TOPK:
        gate = jnp.pad(gate, ((0, 0), (0, 0), (0, 0), (0, MOBA_TOPK - nb)), constant_values=-jnp.inf)
    _, sel = lax.top_k(gate, MOBA_TOPK)
    sel = jnp.minimum(sel, nb - 1)
    sel_ok = jnp.arange(MOBA_TOPK)[None, :] < cur[:, None]
    bi = jnp.arange(B)[:, None, None, None]
    gi = (jnp.arange(MOBA_HEADS) // MOBA_REP)[None, None, :, None]
    ks = kbt[bi, gi, sel]
    vs = vbt[bi, gi, sel]
    ko = kbt[:, :, cur]
    vo = vbt[:, :, cur]
    scale = MOBA_DIM ** -0.5
    s_sel = jnp.einsum('bqhd,bqhtkd->bqhtk', q, ks, preferred_element_type=jnp.float32) * scale
    s_sel = jnp.where(sel_ok[None, :, None, :, None], s_sel, -jnp.inf)
    s_sel = s_sel.reshape(B, Tq, MOBA_HEADS, MOBA_TOPK * MOBA_BLOCK)
    own_pos = cur[:, None] * MOBA_BLOCK + jnp.arange(MOBA_BLOCK)[None, :]
    s_own = jnp.einsum('bqgrd,bgqkd->bqgrk', qg, ko, preferred_element_type=jnp.float32)
    s_own = s_own.reshape(B, Tq, MOBA_HEADS, MOBA_BLOCK) * scale
    s_own = jnp.where((own_pos <= q_pos[:, None])[None, :, None, :], s_own, -jnp.inf)
    p = jax.nn.softmax(jnp.concatenate([s_sel, s_own], axis=-1), axis=-1).astype(q.dtype)
    p_sel = p[..., :MOBA_TOPK * MOBA_BLOCK].reshape(B, Tq, MOBA_HEADS, MOBA_TOPK, MOBA_BLOCK)
    p_own = p[..., MOBA_TOPK * MOBA_BLOCK:].reshape(B, Tq, MOBA_KV_HEADS, MOBA_REP, MOBA_BLOCK)
    o_sel = jnp.einsum('bqhtk,bqhtkd->bqhd', p_sel, vs)
    o_own = jnp.einsum('bqgrk,bgqkd->bqgrd', p_own, vo).reshape(B, Tq, MOBA_HEADS, MOBA_DIM)
    return o_sel + o_own


def moba_prompt(q, k, v):
    B, S = q.shape[:2]
    kbt, vbt, kmean = moba_blocks(k, v)
    nq = S // Q_BLOCK
    qb = q.reshape(B, nq, Q_BLOCK, MOBA_HEADS, MOBA_DIM).swapaxes(0, 1)

    def step(a):
        qblk, p0 = a
        return moba_attend(qblk, p0 + jnp.arange(Q_BLOCK), kbt, vbt, kmean)

    o = lax.map(step, (qb, jnp.arange(nq) * Q_BLOCK))
    return o.swapaxes(0, 1).reshape(B, S, MOBA_HEADS, MOBA_DIM)


def sample_attend(li, qa, c, kpe, qb, kb, vb, page_table, cache_lat, cache_kpe, cache_k, cache_v, w_uk, w_uv, k_norm):
    T = qa.shape[1]
    past = page_table.shape[1] * cache_lat.shape[2]
    q_pos = past + jnp.arange(T)
    k_pos = jnp.arange(past + T)
    causal = k_pos[None, :] <= q_pos[:, None]
    w_uv_h = w_uv.reshape(MLA_KV_LORA, MLA_HEADS, MLA_V)
    scale = MLA_QK ** -0.5

    def one_seq(a):
        pt, qa_b, c_b, kpe_b, qb_b, kb_b, vb_b = a
        c_all = jnp.concatenate([cache_lat[li, pt].reshape(past, MLA_KV_LORA), c_b], axis=0)
        kpe_all = jnp.concatenate([cache_kpe[li, pt].reshape(past, MLA_ROPE), kpe_b], axis=0)
        kn = mla_keys(c_all, kpe_all, w_uk, k_norm)
        s = jnp.einsum('qhd,khd->hqk', qa_b, kn, preferred_element_type=jnp.float32) * scale
        p = jax.nn.softmax(jnp.where(causal[None], s, -jnp.inf), axis=-1).astype(c_all.dtype)
        ctx = jnp.einsum('hqk,kc->qhc', p, c_all)
        oa = jnp.einsum('qhc,chd->qhd', ctx, w_uv_h)
        k_all = jnp.concatenate([cache_k[li, pt].reshape(past, MOBA_KV_HEADS, MOBA_DIM), kb_b], axis=0)
        v_all = jnp.concatenate([cache_v[li, pt].reshape(past, MOBA_KV_HEADS, MOBA_DIM), vb_b], axis=0)
        kbt, vbt, kmean = moba_blocks(k_all[None], v_all[None])
        ob = moba_attend(qb_b[None], q_pos, kbt, vbt, kmean)[0]
        return oa, ob

    return lax.map(one_seq, (page_table, qa, c, kpe, qb, kb, vb))


def conv_layer(x, prev, norm, w_in, conv_w, w_out):
    T = x.shape[1]
    z = jnp.einsum('btd,de->bte', rmsnorm(x, norm), w_in)
    b_g, c_g, h, g = jnp.split(z, 4, axis=-1)
    full = jnp.concatenate([prev, c_g * h], axis=1)
    y = conv_w[0] * full[:, 0:T]
    for j in range(1, CONV_WIDTH):
        y = y + conv_w[j] * full[:, j:j + T]
    out = jnp.einsum('bte,ed->btd', b_g * y * jax.nn.silu(g), w_out)
    return x + out, full[:, -(CONV_WIDTH - 1):]


def setup_inputs(seed: int = 0) -> dict:
    key = jax.random.key(seed)
    ks = jax.random.split(key, 32)
    f32 = jnp.float32
    n_pages = PAST_LEN // PAGE_SIZE
    n_used = DEC_BATCH * n_pages
    n_pool = n_used + n_used // 4

    def nrm(k, shape, scale):
        return jax.random.normal(k, shape, f32) * scale

    def gain(k, shape):
        return 1.0 + 0.02 * jax.random.normal(k, shape, f32)

    page_table = jax.random.permutation(ks[7], n_pool)[:n_used].reshape(DEC_BATCH, n_pages).astype(jnp.int32)
    return {
        'x_prompt': jax.random.normal(ks[0], (BATCH, SEQ, D_MODEL), f32),
        'x_sample': jax.random.normal(ks[1], (DEC_BATCH, DEC_SEQ, D_MODEL), f32),
        'cache_mla_latent': jax.random.normal(ks[2], (N_EVEN, n_pool, PAGE_SIZE, MLA_KV_LORA), f32),
        'cache_mla_kpe': jax.random.normal(ks[3], (N_EVEN, n_pool, PAGE_SIZE, MLA_ROPE), f32),
        'cache_moba_k': jax.random.normal(ks[4], (N_EVEN, n_pool, PAGE_SIZE, MOBA_KV_HEADS, MOBA_DIM), f32),
        'cache_moba_v': jax.random.normal(ks[5], (N_EVEN, n_pool, PAGE_SIZE, MOBA_KV_HEADS, MOBA_DIM), f32),
        'state_conv': jax.random.normal(ks[6], (N_ODD, DEC_BATCH, CONV_WIDTH - 1, CONV_DIM), f32),
        'page_table': page_table,
        'norm_even': gain(ks[8], (N_EVEN, D_MODEL)),
        'w_in_even': nrm(ks[9], (N_EVEN, D_MODEL, EVEN_IN), D_MODEL ** -0.5),
        'mla_qa_norm': gain(ks[10], (N_EVEN, MLA_Q_LORA)),
        'mla_w_uq': nrm(ks[11], (N_EVEN, MLA_Q_LORA, MLA_HEADS * MLA_QK), MLA_Q_LORA ** -0.5),
        'mla_kva_norm': gain(ks[12], (N_EVEN, MLA_KV_LORA)),
        'mla_w_uk': nrm(ks[13], (N_EVEN, MLA_KV_LORA, MLA_HEADS * MLA_NOPE), MLA_KV_LORA ** -0.5),
        'mla_w_uv': nrm(ks[14], (N_EVEN, MLA_KV_LORA, MLA_HEADS * MLA_V), MLA_KV_LORA ** -0.5),
        'mla_q_norm': gain(ks[15], (N_EVEN, MLA_NOPE + MLA_ROPE // 2)),
        'mla_k_norm': gain(ks[16], (N_EVEN, MLA_NOPE + MLA_ROPE // 2)),
        'moba_q_norm': gain(ks[17], (N_EVEN, MOBA_DIM)),
        'moba_k_norm': gain(ks[18], (N_EVEN, MOBA_DIM)),
        'w_out_even': nrm(ks[19], (N_EVEN, WIDTH_A + WIDTH_B, D_MODEL), (WIDTH_A + WIDTH_B) ** -0.5),
        'norm_odd': gain(ks[20], (N_ODD, D_MODEL)),
        'w_in_odd': nrm(ks[21], (N_ODD, D_MODEL, ODD_IN), D_MODEL ** -0.5),
        'conv_w': nrm(ks[22], (N_ODD, CONV_WIDTH, CONV_DIM), CONV_WIDTH ** -0.5),
        'w_out_odd': nrm(ks[23], (N_ODD, CONV_DIM, D_MODEL), CONV_DIM ** -0.5),
    }


def reference(x_prompt, x_sample, cache_mla_latent, cache_mla_kpe, cache_moba_k, cache_moba_v, state_conv,
              page_table, norm_even, w_in_even, mla_qa_norm, mla_w_uq, mla_kva_norm, mla_w_uk, mla_w_uv,
              mla_q_norm, mla_k_norm, moba_q_norm, moba_k_norm, w_out_even, norm_odd, w_in_odd, conv_w, w_out_odd):
    S = x_prompt.shape[1]
    T = x_sample.shape[1]
    past = page_table.shape[1] * cache_mla_latent.shape[2]
    pos_p = jnp.arange(S)
    pos_s = past + jnp.arange(T)
    yp, ys = x_prompt, x_sample
    lat_p, kpe_p, mk_p, mv_p, cv_p = [], [], [], [], []
    lat_s, kpe_s, mk_s, mv_s, cv_s = [], [], [], [], []
    for layer in range(DEPTH):
        i = layer // 2
        if layer % 2 == 0:
            proj = (norm_even[i], w_in_even[i], mla_qa_norm[i], mla_w_uq[i], mla_kva_norm[i],
                    mla_q_norm[i], moba_q_norm[i], moba_k_norm[i])
            qa, c, kpe, qb, kb, vb, ga, gb = even_project(yp, pos_p, *proj)
            kn = mla_keys(c, kpe, mla_w_uk[i], mla_k_norm[i])
            va = jnp.einsum('bsc,chd->bshd', c, mla_w_uv[i].reshape(MLA_KV_LORA, MLA_HEADS, MLA_V))
            oa = mla_prompt_attn(qa, kn, va)
            ob = moba_prompt(qb, kb, vb)
            yp = even_merge(yp, oa, ob, ga, gb, w_out_even[i])
            lat_p.append(c)
            kpe_p.append(kpe)
            mk_p.append(kb)
            mv_p.append(vb)
            qa, c, kpe, qb, kb, vb, ga, gb = even_project(ys, pos_s, *proj)
            oa, ob = sample_attend(i, qa, c, kpe, qb, kb, vb, page_table, cache_mla_latent, cache_mla_kpe,
                                   cache_moba_k, cache_moba_v, mla_w_uk[i], mla_w_uv[i], mla_k_norm[i])
            ys = even_merge(ys, oa, ob, ga, gb, w_out_even[i])
            lat_s.append(c)
            kpe_s.append(kpe)
            mk_s.append(kb)
            mv_s.append(vb)
        else:
            zero_prev = jnp.zeros((yp.shape[0], CONV_WIDTH - 1, CONV_DIM), yp.dtype)
            yp, st_p = conv_layer(yp, zero_prev, norm_odd[i], w_in_odd[i], conv_w[i], w_out_odd[i])
            cv_p.append(st_p)
            ys, st_s = conv_layer(ys, state_conv[i], norm_odd[i], w_in_odd[i], conv_w[i], w_out_odd[i])
            cv_s.append(st_s)
    new_lat_p = jnp.stack(lat_p)
    new_kpe_p = jnp.stack(kpe_p)
    new_mk_p = jnp.stack(mk_p)
    new_mv_p = jnp.stack(mv_p)
    new_conv_p = jnp.stack(cv_p)
    new_lat_s = jnp.stack(lat_s)
    new_kpe_s = jnp.stack(kpe_s)
    new_mk_s = jnp.stack(mk_s)
    new_mv_s = jnp.stack(mv_s)
    new_conv_s = jnp.stack(cv_s)
    return (yp, ys, new_lat_p, new_kpe_p, new_mk_p, new_mv_p, new_conv_p,
            new_lat_s, new_kpe_s, new_mk_s, new_mv_s, new_conv_s)
```

```python
import functools

import jax
import jax.numpy as jnp
import numpy as np
from jax import lax
from jax.experimental import pallas as pl
from jax.experimental.pallas import tpu as pltpu

F32 = jnp.float32
BF16 = jnp.bfloat16

D_MODEL = 2048
MLA_HEADS = 8
MLA_NOPE = 128
MLA_ROPE = 64
MLA_QK = MLA_NOPE + MLA_ROPE
MLA_QK_PAD = 256
MLA_V = 128
MLA_Q_LORA = 512
MLA_KV_LORA = 512
ROPE_THETA = 10000.0
MOBA_HEADS = 8
MOBA_KV_HEADS = 2
MOBA_REP = MOBA_HEADS // MOBA_KV_HEADS
MOBA_DIM = 128
MOBA_BLOCK = 256
MOBA_TOPK = 3
CONV_DIM = D_MODEL
CONV_WIDTH = 3
WIDTH_A = MLA_HEADS * MLA_V
WIDTH_B = MOBA_HEADS * MOBA_DIM
EPS = 1e-6
NEG = -1e30
LANE = 128
SUBLANE = 8
VMEM_LIMIT = 56 * 1024 * 1024

O_QLAT = 0
O_C = O_QLAT + MLA_Q_LORA
O_KPE = O_C + MLA_KV_LORA
O_QB = O_KPE + 2 * LANE
O_KB = O_QB + WIDTH_B
O_VB = O_KB + MOBA_KV_HEADS * MOBA_DIM
O_GA = O_VB + MOBA_KV_HEADS * MOBA_DIM
O_GB = O_GA + WIDTH_A
EVEN_COLS = O_GB + WIDTH_B
UQ_HEAD_COLS = 3 * LANE


def _mm(a, b):
    return jnp.dot(a, b, preferred_element_type=F32)


def _mm_nt(a, b, precision=None):
    return lax.dot_general(a, b, (((1,), (1,)), ((), ())), preferred_element_type=F32, precision=precision)


def _const_spec(shape):
    return pl.BlockSpec(shape, lambda *_: (0,) * len(shape), pipeline_mode=pl.Buffered(1))


def _params(sem):
    return pltpu.CompilerParams(dimension_semantics=sem, vmem_limit_bytes=VMEM_LIMIT)


def _rms(x, width):
    return lax.rsqrt(jnp.sum(x * x, axis=-1, keepdims=True) * (1.0 / width) + EPS)


def _silu(x):
    return x * jax.nn.sigmoid(x)


def _proj_kernel(x_ref, cs_ref, sn_ref, norm_ref, win_ref, qan_ref, wuq_ref, kvan_ref, qg_ref, kg_ref,
                 mqn_ref, mkn_ref, wukv_ref,
                 qa_ref, kn_ref, va_ref, lat_ref, kpe_ref, qb16_ref, qbf_ref, kb_ref, vb_ref, kb16_ref,
                 vb16_ref, kmean_ref, sga_ref, sgb_ref):
    x = x_ref[...]
    xn = ((x * _rms(x, D_MODEL)) * norm_ref[...]).astype(BF16)
    cs = cs_ref[...]
    sn = sn_ref[...]

    ql = _mm(xn, win_ref[:, O_QLAT:O_QLAT + MLA_Q_LORA])
    qln = ((ql * _rms(ql, MLA_Q_LORA)) * qan_ref[...]).astype(BF16)
    qh_all = _mm(qln, wuq_ref[...])
    qg = qg_ref[...]
    q_scale = MLA_QK ** -0.5
    for h in range(MLA_HEADS):
        base = h * UQ_HEAD_COLS
        nope = qh_all[:, base:base + LANE]
        pe = qh_all[:, base + LANE:base + 2 * LANE] * cs + qh_all[:, base + 2 * LANE:base + 3 * LANE] * sn
        ssq = jnp.sum(nope * nope, axis=-1, keepdims=True) + jnp.sum(pe * pe, axis=-1, keepdims=True)
        rr = lax.rsqrt(ssq * (1.0 / MLA_QK) + EPS) * q_scale
        qa_ref[h] = jnp.concatenate([nope * rr * qg[:, :LANE], pe * rr * qg[:, LANE:]], axis=-1).astype(qa_ref.dtype)

    cz = _mm(xn, win_ref[:, O_C:O_C + MLA_KV_LORA])
    c = (cz * _rms(cz, MLA_KV_LORA)) * kvan_ref[...]
    lat_ref[...] = c
    kp = _mm(xn, win_ref[:, O_KPE:O_KPE + 2 * LANE])
    kpe_t = kp[:, :LANE] * cs + kp[:, LANE:] * sn
    kpe_ref[...] = kpe_t[:, :MLA_ROPE]
    kv = _mm(c.astype(BF16), wukv_ref[...])
    va_ref[...] = kv[:, MLA_HEADS * MLA_NOPE:].astype(BF16)
    pes = jnp.sum(kpe_t * kpe_t, axis=-1, keepdims=True)
    kg = kg_ref[...]
    for h in range(MLA_HEADS):
        kh = kv[:, h * MLA_NOPE:(h + 1) * MLA_NOPE]
        rr = lax.rsqrt((jnp.sum(kh * kh, axis=-1, keepdims=True) + pes) * (1.0 / MLA_QK) + EPS)
        kn_ref[h] = jnp.concatenate([kh * rr * kg[:, :LANE], kpe_t * rr * kg[:, LANE:]], axis=-1).astype(BF16)

    qb = _mm(xn, win_ref[:, O_QB:O_QB + WIDTH_B])
    m_scale = MOBA_DIM ** -0.5
    for h in range(MOBA_HEADS):
        t = qb[:, h * MOBA_DIM:(h + 1) * MOBA_DIM]
        t = (t * _rms(t, MOBA_DIM)) * mqn_ref[...]
        qbf_ref[:, h * MOBA_DIM:(h + 1) * MOBA_DIM] = t
        qb16_ref[h] = (t * m_scale).astype(BF16)
    kbz = _mm(xn, win_ref[:, O_KB:O_KB + MOBA_KV_HEADS * MOBA_DIM])
    kparts = []
    for g in range(MOBA_KV_HEADS):
        t = kbz[:, g * MOBA_DIM:(g + 1) * MOBA_DIM]
        kparts.append((t * _rms(t, MOBA_DIM)) * mkn_ref[...])
    kb = jnp.concatenate(kparts, axis=-1)
    kb_ref[...] = kb
    kb16_ref[...] = kb.astype(BF16)
    kmean_ref[0] = jnp.mean(kb, axis=0, keepdims=True)
    vb = _mm(xn, win_ref[:, O_VB:O_VB + MOBA_KV_HEADS * MOBA_DIM])
    vb_ref[...] = vb
    vb16_ref[...] = vb.astype(BF16)

    sga_ref[...] = _silu(_mm(xn, win_ref[:, O_GA:O_GA + WIDTH_A])).astype(BF16)
    sgb_ref[...] = _silu(_mm(xn, win_ref[:, O_GB:O_GB + WIDTH_B])).astype(BF16)


def _even_project(x, cs, sn, w, q_dtype):
    rows = x.shape[0]
    tm = min(MOBA_BLOCK, rows)
    assert rows % tm == 0
    nt = rows // tm
    row = lambda width: pl.BlockSpec((tm, width), lambda i: (i, 0))
    head = lambda width: pl.BlockSpec((MLA_HEADS, tm, width), lambda i: (0, i, 0))
    kvw = MOBA_KV_HEADS * MOBA_DIM
    out_shape = (
        jax.ShapeDtypeStruct((MLA_HEADS, rows, MLA_QK_PAD), q_dtype),
        jax.ShapeDtypeStruct((MLA_HEADS, rows, MLA_QK_PAD), BF16),
        jax.ShapeDtypeStruct((rows, WIDTH_A), BF16),
        jax.ShapeDtypeStruct((rows, MLA_KV_LORA), F32),
        jax.ShapeDtypeStruct((rows, MLA_ROPE), F32),
        jax.ShapeDtypeStruct((MOBA_HEADS, rows, MOBA_DIM), BF16),
        jax.ShapeDtypeStruct((rows, WIDTH_B), F32),
        jax.ShapeDtypeStruct((rows, kvw), F32),
        jax.ShapeDtypeStruct((rows, kvw), F32),
        jax.ShapeDtypeStruct((rows, kvw), BF16),
        jax.ShapeDtypeStruct((rows, kvw), BF16),
        jax.ShapeDtypeStruct((nt, 1, kvw), F32),
        jax.ShapeDtypeStruct((rows, WIDTH_A), BF16),
        jax.ShapeDtypeStruct((rows, WIDTH_B), BF16),
    )
    out_specs = (
        head(MLA_QK_PAD), head(MLA_QK_PAD), row(WIDTH_A), row(MLA_KV_LORA), row(MLA_ROPE),
        head(MOBA_DIM), row(WIDTH_B), row(kvw), row(kvw), row(kvw), row(kvw),
        pl.BlockSpec((1, 1, kvw), lambda i: (i, 0, 0)), row(WIDTH_A), row(WIDTH_B),
    )
    in_specs = [
        row(D_MODEL), row(LANE), row(LANE), _const_spec((1, D_MODEL)), _const_spec((D_MODEL, EVEN_COLS)),
        _const_spec((1, MLA_Q_LORA)), _const_spec((MLA_Q_LORA, MLA_HEADS * UQ_HEAD_COLS)),
        _const_spec((1, MLA_KV_LORA)), _const_spec((1, MLA_QK_PAD)), _const_spec((1, MLA_QK_PAD)),
        _const_spec((1, MOBA_DIM)), _const_spec((1, MOBA_DIM)),
        _const_spec((MLA_KV_LORA, MLA_HEADS * (MLA_NOPE + MLA_V))),
    ]
    return pl.pallas_call(
        _proj_kernel, grid=(nt,), in_specs=in_specs, out_specs=out_specs, out_shape=out_shape,
        compiler_params=_params(("arbitrary",)),
    )(x, cs, sn, w["norm"], w["w_in"], w["qa_norm"], w["w_uq"], w["kva_norm"], w["q_gain"], w["k_gain"],
      w["moba_qn"], w["moba_kn"], w["w_ukv"])


def _gate_kernel(q_ref, km_ref, b_ref, *, nb):
    i = pl.program_id(0)
    nbp = -(-nb // SUBLANE) * SUBLANE
    tq = q_ref.shape[0]
    jrow = lax.broadcasted_iota(jnp.int32, (nbp, tq), 0)
    valid = jrow < i
    for h in range(MOBA_HEADS):
        g = h // MOBA_REP
        km = km_ref[:nbp, g * MOBA_DIM:(g + 1) * MOBA_DIM]
        qh = q_ref[:, h * MOBA_DIM:(h + 1) * MOBA_DIM]
        gate = _mm_nt(km, qh, precision=lax.Precision.HIGHEST)
        gate = jnp.where(valid, gate, -jnp.inf)
        rank = jnp.zeros((nbp, tq), F32)
        for o in range(nb):
            other = gate[o:o + 1, :]
            ahead = (other > gate) | ((other == gate) & (o < jrow))
            rank = rank + jnp.where(ahead, 1.0, 0.0)
        keep = ((rank < MOBA_TOPK) & valid) | (jrow == i)
        bias_t = jnp.where(keep, 0.0, NEG)
        bias_t = jnp.concatenate([bias_t, jnp.zeros((LANE - nbp, tq), F32)], axis=0)
        b_ref[h] = bias_t.T.astype(BF16)


def _moba_gate(qbf, kmean, nb):
    rows = qbf.shape[0]
    assert nb <= LANE and rows == nb * MOBA_BLOCK
    kvw = MOBA_KV_HEADS * MOBA_DIM
    km = jnp.zeros((LANE, kvw), F32).at[:nb].set(kmean.reshape(nb, kvw))
    return pl.pallas_call(
        functools.partial(_gate_kernel, nb=nb), grid=(nb,),
        in_specs=[pl.BlockSpec((MOBA_BLOCK, WIDTH_B), lambda i: (i, 0)), _const_spec((LANE, kvw))],
        out_specs=pl.BlockSpec((MOBA_HEADS, MOBA_BLOCK, LANE), lambda i: (0, i, 0)),
        out_shape=jax.ShapeDtypeStruct((MOBA_HEADS, rows, LANE), BF16),
        compiler_params=_params(("arbitrary",)),
    )(qbf, km)


def _tri_pairs(n):
    ii, jj = [], []
    for i in range(n):
        for j in range(i + 1):
            ii.append(i)
            jj.append(j)
    return jnp.asarray(ii, jnp.int32), jnp.asarray(jj, jnp.int32)


def _online_softmax_step(s, v, m_scr, l_scr, acc_scr):
    m_prev = m_scr[:, :1]
    m_new = jnp.maximum(m_prev, jnp.max(s, axis=-1, keepdims=True))
    alpha = jnp.exp(m_prev - m_new)
    p = jnp.exp(s - m_new)
    l_scr[...] = jnp.broadcast_to(alpha * l_scr[:, :1] + jnp.sum(p, axis=-1, keepdims=True), l_scr.shape)
    acc_scr[...] = alpha * acc_scr[...] + _mm(p.astype(BF16), v)
    m_scr[...] = jnp.broadcast_to(m_new, m_scr.shape)


def _mla_prompt_kernel(ii_ref, jj_ref, q_ref, k_ref, v_ref, o_ref, m_scr, l_scr, acc_scr):
    t = pl.program_id(1)
    i = ii_ref[t]
    j = jj_ref[t]
    tq, tk = q_ref.shape[1], k_ref.shape[1]

    @pl.when(j == 0)
    def _():
        m_scr[...] = jnp.full(m_scr.shape, NEG, F32)
        l_scr[...] = jnp.zeros(l_scr.shape, F32)
        acc_scr[...] = jnp.zeros(acc_scr.shape, F32)

    s = _mm_nt(q_ref[0], k_ref[0])
    qpos = i * tq + lax.broadcasted_iota(jnp.int32, (tq, tk), 0)
    kpos = j * tk + lax.broadcasted_iota(jnp.int32, (tq, tk), 1)
    s = jnp.where(kpos <= qpos, s, NEG)
    _online_softmax_step(s, v_ref[...], m_scr, l_scr, acc_scr)

    @pl.when(j == i)
    def _():
        o_ref[...] = acc_scr[...] / l_scr[:, :1]


def _mla_prompt(qa, kn, va):
    rows = qa.shape[1]
    t = min(512, rows)
    assert rows % t == 0
    n = rows // t
    ii, jj = _tri_pairs(n)
    grid_spec = pltpu.PrefetchScalarGridSpec(
        num_scalar_prefetch=2, grid=(MLA_HEADS, ii.shape[0]),
        in_specs=[
            pl.BlockSpec((1, t, MLA_QK_PAD), lambda h, s, ii, jj: (h, ii[s], 0)),
            pl.BlockSpec((1, t, MLA_QK_PAD), lambda h, s, ii, jj: (h, jj[s], 0)),
            pl.BlockSpec((t, MLA_V), lambda h, s, ii, jj: (jj[s], h)),
        ],
        out_specs=pl.BlockSpec((t, MLA_V), lambda h, s, ii, jj: (ii[s], h)),
        scratch_shapes=[pltpu.VMEM((t, LANE), F32), pltpu.VMEM((t, LANE), F32), pltpu.VMEM((t, MLA_V), F32)],
    )
    return pl.pallas_call(
        _mla_prompt_kernel, grid_spec=grid_spec, out_shape=jax.ShapeDtypeStruct((rows, WIDTH_A), F32),
        compiler_params=_params(("arbitrary", "arbitrary")),
    )(ii, jj, qa, kn, va)


def _moba_prompt_kernel(ii_ref, jj_ref, q_ref, b_ref, k_ref, v_ref, o_ref, m_scr, l_scr, acc_scr):
    t = pl.program_id(1)
    i = ii_ref[t]
    j = jj_ref[t]
    blk = MOBA_BLOCK
    nrow = MOBA_REP * blk

    @pl.when(j == 0)
    def _():
        m_scr[...] = jnp.full(m_scr.shape, NEG, F32)
        l_scr[...] = jnp.zeros(l_scr.shape, F32)
        acc_scr[...] = jnp.zeros(acc_scr.shape, F32)

    q_aug = jnp.concatenate([q_ref[...].reshape(nrow, MOBA_DIM), b_ref[...].reshape(nrow, LANE)], axis=-1)
    onehot = jnp.where(lax.broadcasted_iota(jnp.int32, (blk, LANE), 1) == j, 1.0, 0.0).astype(BF16)
    k_aug = jnp.concatenate([k_ref[...], onehot], axis=-1)
    s = _mm_nt(q_aug, k_aug)
    qpos = lax.broadcasted_iota(jnp.int32, (MOBA_REP, blk, blk), 1).reshape(nrow, blk)
    kpos = lax.broadcasted_iota(jnp.int32, (nrow, blk), 1)
    s = jnp.where((j < i) | (kpos <= qpos), s, NEG)
    _online_softmax_step(s, v_ref[...], m_scr, l_scr, acc_scr)

    @pl.when(j == i)
    def _():
        o = acc_scr[...] / l_scr[:, :1]
        for r in range(MOBA_REP):
            o_ref[:, r * MOBA_DIM:(r + 1) * MOBA_DIM] = o[r * blk:(r + 1) * blk]


def _moba_prompt(qb16, bias, kb16, vb16):
    rows = qb16.shape[1]
    nb = rows // MOBA_BLOCK
    ii, jj = _tri_pairs(nb)
    nrow = MOBA_REP * MOBA_BLOCK
    grid_spec = pltpu.PrefetchScalarGridSpec(
        num_scalar_prefetch=2, grid=(MOBA_KV_HEADS, ii.shape[0]),
        in_specs=[
            pl.BlockSpec((MOBA_REP, MOBA_BLOCK, MOBA_DIM), lambda g, s, ii, jj: (g, ii[s], 0)),
            pl.BlockSpec((MOBA_REP, MOBA_BLOCK, LANE), lambda g, s, ii, jj: (g, ii[s], 0)),
            pl.BlockSpec((MOBA_BLOCK, MOBA_DIM), lambda g, s, ii, jj: (jj[s], g)),
            pl.BlockSpec((MOBA_BLOCK, MOBA_DIM), lambda g, s, ii, jj: (jj[s], g)),
        ],
        out_specs=pl.BlockSpec((MOBA_BLOCK, MOBA_REP * MOBA_DIM), lambda g, s, ii, jj: (ii[s], g)),
        scratch_shapes=[pltpu.VMEM((nrow, LANE), F32), pltpu.VMEM((nrow, LANE), F32),
                        pltpu.VMEM((nrow, MOBA_DIM), F32)],
    )
    return pl.pallas_call(
        _moba_prompt_kernel, grid_spec=grid_spec, out_shape=jax.ShapeDtypeStruct((rows, WIDTH_B), F32),
        compiler_params=_params(("arbitrary", "arbitrary")),
    )(ii, jj, qb16, bias, kb16, vb16)


PAGES_PER_STEP = 8
SUB_KEYS = 512


def _paged_copies(pt_ref, srcs, bufs, sem, b, j, slot, page_size):
    copies = []
    for p in range(PAGES_PER_STEP):
        page = pt_ref[b, j * PAGES_PER_STEP + p]
        for a, (src, buf) in enumerate(zip(srcs, bufs)):
            copies.append(pltpu.make_async_copy(
                src.at[page], buf.at[slot, pl.ds(p * page_size, page_size)], sem.at[a, slot]))
    return copies


def _paged_pipeline(pt_ref, srcs, bufs, sem, page_size):
    b = pl.program_id(0)
    j = pl.program_id(1)
    nj = pl.num_programs(1)
    step = b * nj + j
    slot = step % 2

    @pl.when(step == 0)
    def _():
        for cp in _paged_copies(pt_ref, srcs, bufs, sem, b, j, slot, page_size):
            cp.start()

    @pl.when(step + 1 < pl.num_programs(0) * nj)
    def _():
        last = j + 1 == nj
        nb_ = jnp.where(last, b + 1, b)
        nj_ = jnp.where(last, 0, j + 1)
        for cp in _paged_copies(pt_ref, srcs, bufs, sem, nb_, nj_, 1 - slot, page_size):
            cp.start()

    for cp in _paged_copies(pt_ref, srcs, bufs, sem, b, j, slot, page_size):
        cp.wait()
    return slot


def _mla_paged_kernel(pt_ref, qbd_ref, qpe_ref, wukt_ref, wuv_ref, cnew_ref, knew_ref, lat_hbm, kpe_hbm,
                      o_ref, lat_buf, kpe_buf, sem, lhs_scr, m_scr, l_scr, ctx_scr, *, page_size):
    b = pl.program_id(0)
    j = pl.program_id(1)
    nj = pl.num_programs(1)
    nq = qbd_ref.shape[1]
    nhd = MLA_HEADS * MLA_NOPE

    @pl.when((b == 0) & (j == 0))
    def _():
        lhs_scr[:nhd, :] = wukt_ref[...]

    @pl.when(j == 0)
    def _():
        lhs_scr[nhd:, :] = _mm(qbd_ref[0], wukt_ref[...]).astype(BF16)
        m_scr[...] = jnp.full(m_scr.shape, NEG, F32)
        l_scr[...] = jnp.zeros(l_scr.shape, F32)
        ctx_scr[...] = jnp.zeros(ctx_scr.shape, F32)

    slot = _paged_pipeline(pt_ref, (lat_hbm, kpe_hbm), (lat_buf, kpe_buf), sem, page_size)

    def process(c, kpe, mask):
        n = c.shape[0]
        cb = c.astype(BF16)
        res = _mm_nt(lhs_scr[...], cb)
        kn = res[:nhd]
        n2 = jnp.sum((kn * kn).reshape(MLA_HEADS, MLA_NOPE, n), axis=1)
        k2 = kpe * kpe
        hi = k2.astype(BF16)
        lo = (k2 - hi.astype(F32)).astype(BF16)
        pe2 = _mm_nt(jnp.ones((SUBLANE, 2 * MLA_ROPE), BF16), jnp.concatenate([hi, lo], axis=-1))
        spe = _mm_nt(qpe_ref[0], kpe.astype(BF16))
        r = lax.rsqrt((n2 + pe2) * (1.0 / MLA_QK) + EPS)
        s = ((res[nhd:] + spe).reshape(nq // MLA_HEADS, MLA_HEADS, n) * r[None]).reshape(nq, n)
        if mask is not None:
            s = jnp.where(mask, s, NEG)
        _online_softmax_step(s, cb, m_scr, l_scr, ctx_scr)

    keys = PAGES_PER_STEP * page_size
    sub = min(SUB_KEYS, keys)
    for u in range(keys // sub):
        process(lat_buf[slot, u * sub:(u + 1) * sub, :], kpe_buf[slot, u * sub:(u + 1) * sub, :], None)

    @pl.when(j == nj - 1)
    def _():
        nnew = cnew_ref.shape[1]
        qidx = lax.broadcasted_iota(jnp.int32, (nq // MLA_HEADS, MLA_HEADS, nnew), 0).reshape(nq, nnew)
        kidx = lax.broadcasted_iota(jnp.int32, (nq, nnew), 1)
        process(cnew_ref[0], knew_ref[0], kidx <= qidx)
        ctx = (ctx_scr[...] / l_scr[:, :1]).astype(BF16)
        full = _mm(ctx, wuv_ref[...])
        hrow = lax.broadcasted_iota(jnp.int32, (nq // MLA_HEADS, MLA_HEADS, MLA_V), 1).reshape(nq, MLA_V)
        out = jnp.zeros((nq, MLA_V), F32)
        for h in range(MLA_HEADS):
            out = out + jnp.where(hrow == h, full[:, h * MLA_V:(h + 1) * MLA_V], 0.0)
        o_ref[0] = out


def _mla_paged(page_table, qbd, qpe, wukt, wuv, cnew, knew, cache_lat, cache_kpe):
    ns, n_pages = page_table.shape
    page_size = cache_lat.shape[1]
    assert n_pages % PAGES_PER_STEP == 0
    nq = qbd.shape[1]
    keys = PAGES_PER_STEP * page_size
    nhd = MLA_HEADS * MLA_NOPE
    per_seq = lambda shape: pl.BlockSpec((1,) + shape, lambda b, j, pt: (b, 0, 0))
    grid_spec = pltpu.PrefetchScalarGridSpec(
        num_scalar_prefetch=1, grid=(ns, n_pages // PAGES_PER_STEP),
        in_specs=[
            per_seq((nq, nhd)), per_seq((nq, MLA_ROPE)),
            _const_spec((nhd, MLA_KV_LORA)), _const_spec((MLA_KV_LORA, MLA_HEADS * MLA_V)),
            per_seq(cnew.shape[1:]), per_seq(knew.shape[1:]),
            pl.BlockSpec(memory_space=pl.ANY), pl.BlockSpec(memory_space=pl.ANY),
        ],
        out_specs=per_seq((nq, MLA_V)),
        scratch_shapes=[
            pltpu.VMEM((2, keys, MLA_KV_LORA), F32), pltpu.VMEM((2, keys, MLA_ROPE), F32),
            pltpu.SemaphoreType.DMA((2, 2)),
            pltpu.VMEM((nhd + nq, MLA_KV_LORA), BF16),
            pltpu.VMEM((nq, LANE), F32), pltpu.VMEM((nq, LANE), F32), pltpu.VMEM((nq, MLA_KV_LORA), F32),
        ],
    )
    return pl.pallas_call(
        functools.partial(_mla_paged_kernel, page_size=page_size), grid_spec=grid_spec,
        out_shape=jax.ShapeDtypeStruct((ns, nq, MLA_V), F32),
        compiler_params=_params(("arbitrary", "arbitrary")),
    )(page_table, qbd, qpe, wukt, wuv, cnew, knew, cache_lat, cache_kpe)


def _moba_paged_kernel(pt_ref, q16_ref, qf_ref, knew_ref, vnew_ref, k_hbm, v_hbm, o_ref,
                       k_buf, v_buf, sem, m_s, l_s, g_s, o_s, *, page_size, nb):
    j = pl.program_id(1)
    nj = pl.num_programs(1)
    nq = q16_ref.shape[1]
    kvw = MOBA_KV_HEADS * MOBA_DIM
    blocks_per_step = PAGES_PER_STEP * page_size // MOBA_BLOCK

    slot = _paged_pipeline(pt_ref, (k_hbm, v_hbm), (k_buf, v_buf), sem, page_size)

    q16 = q16_ref[0]
    qf = qf_ref[0]
    wide = (nq, LANE)
    for u in range(blocks_per_step):
        jb = j * blocks_per_step + u
        kf = k_buf[slot, u * MOBA_BLOCK:(u + 1) * MOBA_BLOCK, :]
        vf = v_buf[slot, u * MOBA_BLOCK:(u + 1) * MOBA_BLOCK, :]
        kmean = jnp.mean(kf, axis=0, keepdims=True)
        g_s[jb] = jnp.broadcast_to(jnp.sum(qf * kmean, axis=-1, keepdims=True), wide)
        s = _mm_nt(q16, kf.astype(BF16))
        m = jnp.max(s, axis=-1, keepdims=True)
        p = jnp.exp(s - m)
        m_s[jb] = jnp.broadcast_to(m, wide)
        l_s[jb] = jnp.broadcast_to(jnp.sum(p, axis=-1, keepdims=True), wide)
        o_s[jb] = _mm(p.astype(BF16), vf.astype(BF16))

    @pl.when(j == nj - 1)
    def _():
        lane = lax.broadcasted_iota(jnp.int32, wide, 1)

        def gather(jb, carry):
            gm, mm, lm = carry
            hit = lane == jb
            return jnp.where(hit, g_s[jb], gm), jnp.where(hit, m_s[jb], mm), jnp.where(hit, l_s[jb], lm)
        gm, mm, lm = lax.fori_loop(
            0, nb, gather, (jnp.full(wide, -jnp.inf, F32), jnp.full(wide, NEG, F32), jnp.zeros(wide, F32)))

        def count(o, rank):
            other = g_s[o]
            ahead = (other > gm) | ((other == gm) & (o < lane))
            return rank + jnp.where(ahead, 1.0, 0.0)
        rank = lax.fori_loop(0, nb, count, jnp.zeros(wide, F32))
        keep = (rank < MOBA_TOPK) & (lane < nb)

        nnew = knew_ref.shape[1]
        qidx = lax.broadcasted_iota(jnp.int32, (nq // MOBA_HEADS, MOBA_HEADS, nnew), 0).reshape(nq, nnew)
        kidx = lax.broadcasted_iota(jnp.int32, (nq, nnew), 1)
        s_own = jnp.where(kidx <= qidx, _mm_nt(q16, knew_ref[0].astype(BF16)), NEG)
        m_own = jnp.max(s_own, axis=-1, keepdims=True)
        m_tot = jnp.maximum(jnp.max(jnp.where(keep, mm, NEG), axis=-1, keepdims=True), m_own)
        p_own = jnp.exp(s_own - m_tot)
        wgt = jnp.exp(jnp.where(keep, mm - m_tot, NEG))
        l_tot = jnp.sum(wgt * lm, axis=-1, keepdims=True) + jnp.sum(p_own, axis=-1, keepdims=True)
        acc0 = _mm(p_own.astype(BF16), vnew_ref[0].astype(BF16))

        def combine(jb, acc):
            w_col = jnp.sum(jnp.where(lane == jb, wgt, 0.0), axis=-1, keepdims=True)
            return acc + w_col * o_s[jb]
        acc = lax.fori_loop(0, nb, combine, acc0) / l_tot
        grow = lax.broadcasted_iota(jnp.int32, (nq // MOBA_HEADS, MOBA_HEADS, MOBA_DIM), 1).reshape(nq, MOBA_DIM)
        o_ref[0] = jnp.where(grow < MOBA_REP, acc[:, :MOBA_DIM], acc[:, MOBA_DIM:])


def _moba_paged(page_table, q16, qf, knew, vnew, cache_k, cache_v):
    ns, n_pages = page_table.shape
    page_size = cache_k.shape[1]
    past = n_pages * page_size
    assert n_pages % PAGES_PER_STEP == 0 and past % MOBA_BLOCK == 0
    assert (PAGES_PER_STEP * page_size) % MOBA_BLOCK == 0
    nb = past // MOBA_BLOCK
    assert nb <= LANE
    nq = q16.shape[1]
    kvw = MOBA_KV_HEADS * MOBA_DIM
    keys = PAGES_PER_STEP * page_size
    per_seq = lambda shape: pl.BlockSpec((1,) + shape, lambda b, j, pt: (b, 0, 0))
    grid_spec = pltpu.PrefetchScalarGridSpec(
        num_scalar_prefetch=1, grid=(ns, n_pages // PAGES_PER_STEP),
        in_specs=[
            per_seq((nq, kvw)), per_seq((nq, kvw)), per_seq(knew.shape[1:]), per_seq(vnew.shape[1:]),
            pl.BlockSpec(memory_space=pl.ANY), pl.BlockSpec(memory_space=pl.ANY),
        ],
        out_specs=per_seq((nq, MOBA_DIM)),
        scratch_shapes=[
            pltpu.VMEM((2, keys, kvw), F32), pltpu.VMEM((2, keys, kvw), F32),
            pltpu.SemaphoreType.DMA((2, 2)),
            pltpu.VMEM((nb, nq, LANE), F32), pltpu.VMEM((nb, nq, LANE), F32), pltpu.VMEM((nb, nq, LANE), F32),
            pltpu.VMEM((nb, nq, kvw), F32),
        ],
    )
    return pl.pallas_call(
        functools.partial(_moba_paged_kernel, page_size=page_size, nb=nb), grid_spec=grid_spec,
        out_shape=jax.ShapeDtypeStruct((ns, nq, MOBA_DIM), F32),
        compiler_params=_params(("arbitrary", "arbitrary")),
    )(page_table, q16, qf, knew, vnew, cache_k, cache_v)


def _merge_kernel(x_ref, oa_ref, ob_ref, sga_ref, sgb_ref, w_ref, o_ref):
    mix = jnp.concatenate([(oa_ref[...] * sga_ref[...].astype(F32)).astype(BF16),
                           (ob_ref[...] * sgb_ref[...].astype(F32)).astype(BF16)], axis=-1)
    o_ref[...] = x_ref[...] + _mm(mix, w_ref[...])


def _even_merge(x, oa, ob, sga, sgb, w_out):
    rows = x.shape[0]
    tm = min(256, rows)
    assert rows % tm == 0
    row = lambda width: pl.BlockSpec((tm, width), lambda i: (i, 0))
    return pl.pallas_call(
        _merge_kernel, grid=(rows // tm,),
        in_specs=[row(D_MODEL), row(WIDTH_A), row(WIDTH_B), row(WIDTH_A), row(WIDTH_B),
                  _const_spec((WIDTH_A + WIDTH_B, D_MODEL))],
        out_specs=row(D_MODEL), out_shape=jax.ShapeDtypeStruct((rows, D_MODEL), F32),
        compiler_params=_params(("arbitrary",)),
    )(x, oa, ob, sga, sgb, w_out)


def _conv_kernel(x_ref, norm_ref, wb_ref, wc_ref, wh_ref, wg_ref, cw_ref, wo_ref, prev_ref,
                 o_ref, st_ref, xn_scr, acc_scr, carry_scr, *, step_major_seqs):
    i = pl.program_id(0)
    n = pl.program_id(1)
    tm = x_ref.shape[0]

    @pl.when(n == 0)
    def _():
        x = x_ref[...]
        xn_scr[...] = ((x * _rms(x, D_MODEL)) * norm_ref[...]).astype(BF16)
        acc_scr[...] = x

    xn = xn_scr[...]
    ch = _mm(xn, wc_ref[...]) * _mm(xn, wh_ref[...])
    cw = cw_ref[...]
    if step_major_seqs:
        ns = step_major_seqs
        full = jnp.concatenate([prev_ref[...], ch], axis=0)
        conv = cw[0:1] * full[0:tm] + cw[1:2] * full[ns:ns + tm] + cw[2:3] * full[2 * ns:2 * ns + tm]
        st_ref[...] = ch[tm - 2 * ns:]
    else:
        @pl.when(i == 0)
        def _():
            carry_scr[n] = jnp.zeros(carry_scr.shape[1:], F32)

        prev = carry_scr[n]
        rowi = lax.broadcasted_iota(jnp.int32, ch.shape, 0)
        m1 = jnp.where(rowi == 0, prev[SUBLANE - 1:SUBLANE], pltpu.roll(ch, 1, 0))
        m2 = jnp.where(rowi == 0, prev[SUBLANE - 2:SUBLANE - 1],
                       jnp.where(rowi == 1, prev[SUBLANE - 1:SUBLANE], pltpu.roll(ch, 2, 0)))
        conv = cw[0:1] * m2 + cw[1:2] * m1 + cw[2:3] * ch
        carry_scr[n] = ch[tm - SUBLANE:]
        st_ref[...] = ch[tm - SUBLANE:]
    y = (_mm(xn, wb_ref[...]) * conv * _silu(_mm(xn, wg_ref[...]))).astype(BF16)
    acc_scr[...] += _mm(y, wo_ref[...])

    @pl.when(n == pl.num_programs(1) - 1)
    def _():
        o_ref[...] = acc_scr[...]


def _conv_layer(x, prev, w, step_major_seqs):
    rows = x.shape[0]
    tm = rows if step_major_seqs else min(512, rows)
    tn = 512
    assert rows % tm == 0
    ncol = CONV_DIM // tn
    st_rows = prev.shape[0] if step_major_seqs else SUBLANE
    wcol = lambda k: pl.BlockSpec((D_MODEL, tn), lambda i, n, k=k: (0, n + k * ncol))
    return pl.pallas_call(
        functools.partial(_conv_kernel, step_major_seqs=step_major_seqs), grid=(rows // tm, ncol),
        in_specs=[
            pl.BlockSpec((tm, D_MODEL), lambda i, n: (i, 0)), _const_spec((1, D_MODEL)),
            wcol(0), wcol(1), wcol(2), wcol(3),
            pl.BlockSpec((CONV_WIDTH, tn), lambda i, n: (0, n)),
            pl.BlockSpec((tn, D_MODEL), lambda i, n: (n, 0)),
            pl.BlockSpec((prev.shape[0], tn), lambda i, n: (0, n)),
        ],
        out_specs=(pl.BlockSpec((tm, D_MODEL), lambda i, n: (i, 0)),
                   pl.BlockSpec((st_rows, tn), lambda i, n: (i, n))),
        out_shape=(jax.ShapeDtypeStruct((rows, D_MODEL), F32),
                   jax.ShapeDtypeStruct((rows // tm * st_rows, CONV_DIM), F32)),
        scratch_shapes=[pltpu.VMEM((tm, D_MODEL), BF16), pltpu.VMEM((tm, D_MODEL), F32),
                        pltpu.VMEM((ncol, SUBLANE, tn), F32)],
        compiler_params=_params(("arbitrary", "arbitrary")),
    )(x, w["norm"], w["w_in"], w["w_in"], w["w_in"], w["w_in"], w["conv_w"], w["w_out"], prev)


def _rot_cols(w):
    half = MLA_ROPE // 2
    return jnp.concatenate([-w[..., half:], w[..., :half]], axis=-1)


def _pad_cols(w, width):
    return jnp.pad(w, ((0, 0), (0, width - w.shape[-1])))


def _gain_pad(g):
    full = jnp.concatenate([g[:MLA_NOPE], g[MLA_NOPE:], g[MLA_NOPE:]])
    return jnp.pad(full, (0, MLA_QK_PAD - MLA_QK)).reshape(1, MLA_QK_PAD)


def _even_weights(norm, w_in, qa_norm, w_uq, kva_norm, w_uk, w_uv, q_norm, k_norm, moba_qn, moba_kn):
    o = np.cumsum((0, MLA_Q_LORA, MLA_KV_LORA, MLA_ROPE, WIDTH_B, MOBA_KV_HEADS * MOBA_DIM,
                   MOBA_KV_HEADS * MOBA_DIM, WIDTH_A, WIDTH_B))
    kpe_w = w_in[:, o[2]:o[3]]
    w_in2 = jnp.concatenate([
        w_in[:, o[0]:o[2]], _pad_cols(kpe_w, LANE), _pad_cols(_rot_cols(kpe_w), LANE), w_in[:, o[3]:],
    ], axis=-1).astype(BF16)
    uq = w_uq.reshape(MLA_Q_LORA, MLA_HEADS, MLA_QK)
    pe_w = uq[..., MLA_NOPE:]
    pad = jnp.zeros((MLA_Q_LORA, MLA_HEADS, LANE - MLA_ROPE), F32)
    uq2 = jnp.concatenate([uq[..., :MLA_NOPE], pe_w, pad, _rot_cols(pe_w), pad], axis=-1)
    return {
        "norm": norm.reshape(1, D_MODEL), "w_in": w_in2, "qa_norm": qa_norm.reshape(1, MLA_Q_LORA),
        "w_uq": uq2.reshape(MLA_Q_LORA, MLA_HEADS * UQ_HEAD_COLS).astype(BF16),
        "kva_norm": kva_norm.reshape(1, MLA_KV_LORA), "q_gain": _gain_pad(q_norm), "k_gain": _gain_pad(k_norm),
        "moba_qn": moba_qn.reshape(1, MOBA_DIM), "moba_kn": moba_kn.reshape(1, MOBA_DIM),
        "w_ukv": jnp.concatenate([w_uk, w_uv], axis=-1).astype(BF16),
    }


def _rope_tables(pos):
    half = MLA_ROPE // 2
    inv = ROPE_THETA ** (-jnp.arange(half, dtype=F32) / half)
    ang = pos.astype(F32)[:, None] * inv
    cos, sin = jnp.cos(ang), jnp.sin(ang)
    pad = jnp.zeros((pos.shape[0], LANE - MLA_ROPE), F32)
    return jnp.concatenate([cos, cos, pad], axis=-1), jnp.concatenate([sin, sin, pad], axis=-1)


def _block_diag_rows(x, nblk):
    eye = jnp.eye(nblk, dtype=x.dtype)
    out = x[..., :, None, :] * eye[:, :, None]
    return out.reshape(x.shape[:-1] + (nblk * x.shape[-1],))


def kernel(x_prompt, x_sample, cache_mla_latent, cache_mla_kpe, cache_moba_k, cache_moba_v, state_conv, page_table, norm_even, w_in_even, mla_qa_norm, mla_w_uq, mla_kva_norm, mla_w_uk, mla_w_uv, mla_q_norm, mla_k_norm, moba_q_norm, moba_k_norm, w_out_even, norm_odd, w_in_odd, conv_w, w_out_odd):
    assert x_prompt.shape[0] == 1 and norm_even.shape[0] == 1 and norm_odd.shape[0] == 1
    seq = x_prompt.shape[1]
    ns, steps = x_sample.shape[:2]
    n_pages = page_table.shape[1]
    page_size = cache_mla_latent.shape[2]
    past = n_pages * page_size
    assert seq % MOBA_BLOCK == 0 and steps <= SUBLANE
    kvw = MOBA_KV_HEADS * MOBA_DIM

    we = _even_weights(norm_even[0], w_in_even[0], mla_qa_norm[0], mla_w_uq[0], mla_kva_norm[0], mla_w_uk[0],
                       mla_w_uv[0], mla_q_norm[0], mla_k_norm[0], moba_q_norm[0], moba_k_norm[0])
    w_out_e = w_out_even[0].astype(BF16)
    wo = {"norm": norm_odd[0].reshape(1, D_MODEL), "w_in": w_in_odd[0].astype(BF16), "conv_w": conv_w[0],
          "w_out": w_out_odd[0].astype(BF16)}

    xp = x_prompt[0]
    cs, sn = _rope_tables(jnp.arange(seq))
    (qa, kn, va, lat_p, kpe_p, qb16, qbf, kb_p, vb_p, kb16, vb16, kmean, sga, sgb) = _even_project(xp, cs, sn, we, BF16)
    oa = _mla_prompt(qa, kn, va)
    bias = _moba_gate(qbf, kmean, seq // MOBA_BLOCK)
    ob = _moba_prompt(qb16, bias, kb16, vb16)
    y1p = _even_merge(xp, oa, ob, sga, sgb, w_out_e)
    y2p, st_p = _conv_layer(y1p, jnp.zeros((SUBLANE, CONV_DIM), F32), wo, 0)

    rows_s = steps * ns
    xs = x_sample.transpose(1, 0, 2).reshape(rows_s, D_MODEL)
    cs_s, sn_s = _rope_tables(past + jnp.repeat(jnp.arange(steps), ns))
    (qa_s, _, _, lat_s, kpe_s, qb16_s, qbf_s, kb_s, vb_s, _, _, _, sga_s, sgb_s) = _even_project(xs, cs_s, sn_s, we, F32)

    def per_seq(a):
        return a.reshape(steps, ns, a.shape[-1]).transpose(1, 0, 2)

    def pad_new(a):
        return jnp.pad(a, ((0, 0), (0, LANE - steps), (0, 0)))

    nq = steps * MLA_HEADS
    qk = qa_s.reshape(MLA_HEADS, steps, ns, MLA_QK_PAD).transpose(2, 1, 0, 3) * we["k_gain"][0]
    qbd = _block_diag_rows(qk[..., :MLA_NOPE], MLA_HEADS).reshape(ns, nq, MLA_HEADS * MLA_NOPE).astype(BF16)
    qpe = qk[..., MLA_NOPE:MLA_QK].reshape(ns, nq, MLA_ROPE).astype(BF16)
    oa_s = _mla_paged(page_table, qbd, qpe, mla_w_uk[0].T.astype(BF16), mla_w_uv[0].astype(BF16),
                      pad_new(per_seq(lat_s)), pad_new(per_seq(kpe_s)), cache_mla_latent[0], cache_mla_kpe[0])

    def group_diag(a):
        g = a.reshape(ns, steps, MOBA_KV_HEADS, MOBA_REP, MOBA_DIM).transpose(0, 1, 3, 2, 4)
        return _block_diag_rows(g, MOBA_KV_HEADS).transpose(0, 1, 3, 2, 4).reshape(ns, nq, kvw)

    q16_s = group_diag(qb16_s.reshape(MOBA_HEADS, steps, ns, MOBA_DIM).transpose(2, 1, 0, 3))
    qf_s = group_diag(per_seq(qbf_s).reshape(ns, steps, MOBA_HEADS, MOBA_DIM))
    ob_s = _moba_paged(page_table, q16_s, qf_s, pad_new(per_seq(kb_s)), pad_new(per_seq(vb_s)),
                       cache_moba_k[0].reshape(-1, page_size, kvw), cache_moba_v[0].reshape(-1, page_size, kvw))

    def step_major(o):
        return o.reshape(ns, steps, -1).transpose(1, 0, 2).reshape(rows_s, -1)

    y1s = _even_merge(xs, step_major(oa_s), step_major(ob_s), sga_s, sgb_s, w_out_e)
    prev_s = state_conv[0].transpose(1, 0, 2).reshape((CONV_WIDTH - 1) * ns, CONV_DIM)
    y2s, st_s = _conv_layer(y1s, prev_s, wo, ns)

    return (
        y2p[None], per_seq(y2s),
        lat_p[None, None], kpe_p[None, None],
        kb_p.reshape(1, 1, seq, MOBA_KV_HEADS, MOBA_DIM), vb_p.reshape(1, 1, seq, MOBA_KV_HEADS, MOBA_DIM),
        st_p[st_p.shape[0] - (CONV_WIDTH - 1):][None, None],
        per_seq(lat_s)[None], per_seq(kpe_s)[None],
        per_seq(kb_s).reshape(1, ns, steps, MOBA_KV_HEADS, MOBA_DIM),
        per_seq(vb_s).reshape(1, ns, steps, MOBA_KV_HEADS, MOBA_DIM),
        st_s.reshape(CONV_WIDTH - 1, ns, CONV_DIM).transpose(1, 0, 2)[None],
    )
```

```python
import functools

import jax
import jax.numpy as jnp
import numpy as np
from jax import lax
from jax.experimental import pallas as pl
from jax.experimental.pallas import tpu as pltpu

F32 = jnp.float32
BF16 = jnp.bfloat16

D_MODEL = 2048
MLA_HEADS = 8
MLA_NOPE = 128
MLA_ROPE = 64
MLA_QK = MLA_NOPE + MLA_ROPE
MLA_QK_PAD = 256
MLA_V = 128
MLA_Q_LORA = 512
MLA_KV_LORA = 512
ROPE_THETA = 10000.0
MOBA_HEADS = 8
MOBA_KV_HEADS = 2
MOBA_REP = MOBA_HEADS // MOBA_KV_HEADS
MOBA_DIM = 128
MOBA_BLOCK = 256
MOBA_TOPK = 3
CONV_DIM = D_MODEL
CONV_WIDTH = 3
WIDTH_A = MLA_HEADS * MLA_V
WIDTH_B = MOBA_HEADS * MOBA_DIM
EPS = 1e-6
NEG = -1e30
LANE = 128
SUBLANE = 8
VMEM_LIMIT = 56 * 1024 * 1024

O_QLAT = 0
O_C = O_QLAT + MLA_Q_LORA
O_KPE = O_C + MLA_KV_LORA
O_QB = O_KPE + 2 * LANE
O_KB = O_QB + WIDTH_B
O_VB = O_KB + MOBA_KV_HEADS * MOBA_DIM
O_GA = O_VB + MOBA_KV_HEADS * MOBA_DIM
O_GB = O_GA + WIDTH_A
EVEN_COLS = O_GB + WIDTH_B
UQ_HEAD_COLS = 3 * LANE


def _mm(a, b):
    return jnp.dot(a, b, preferred_element_type=F32)


def _mm_nt(a, b, precision=None):
    return lax.dot_general(a, b, (((1,), (1,)), ((), ())), preferred_element_type=F32, precision=precision)


def _const_spec(shape):
    return pl.BlockSpec(shape, lambda *_: (0,) * len(shape), pipeline_mode=pl.Buffered(1))


def _params(sem):
    return pltpu.CompilerParams(dimension_semantics=sem, vmem_limit_bytes=VMEM_LIMIT)


def _rms(x, width):
    return lax.rsqrt(jnp.sum(x * x, axis=-1, keepdims=True) * (1.0 / width) + EPS)


def _silu(x):
    return x * jax.nn.sigmoid(x)


def _proj_kernel(x_ref, cs_ref, sn_ref, norm_ref, win_ref, qan_ref, wuq_ref, kvan_ref, qg_ref, kg_ref,
                 mqn_ref, mkn_ref, wukv_ref,
                 qa_ref, kn_ref, va_ref, lat_ref, kpe_ref, qb16_ref, qbf_ref, kb_ref, vb_ref, kb16_ref,
                 vb16_ref, kmean_ref, sga_ref, sgb_ref):
    x = x_ref[...]
    xn = ((x * _rms(x, D_MODEL)) * norm_ref[...]).astype(BF16)
    cs = cs_ref[...]
    sn = sn_ref[...]

    ql = _mm(xn, win_ref[:, O_QLAT:O_QLAT + MLA_Q_LORA])
    qln = ((ql * _rms(ql, MLA_Q_LORA)) * qan_ref[...]).astype(BF16)
    qh_all = _mm(qln, wuq_ref[...])
    qg = qg_ref[...]
    q_scale = MLA_QK ** -0.5
    for h in range(MLA_HEADS):
        base = h * UQ_HEAD_COLS
        nope = qh_all[:, base:base + LANE]
        pe = qh_all[:, base + LANE:base + 2 * LANE] * cs + qh_all[:, base + 2 * LANE:base + 3 * LANE] * sn
        ssq = jnp.sum(nope * nope, axis=-1, keepdims=True) + jnp.sum(pe * pe, axis=-1, keepdims=True)
        rr = lax.rsqrt(ssq * (1.0 / MLA_QK) + EPS) * q_scale
        qa_ref[h] = jnp.concatenate([nope * rr * qg[:, :LANE], pe * rr * qg[:, LANE:]], axis=-1).astype(qa_ref.dtype)

    cz = _mm(xn, win_ref[:, O_C:O_C + MLA_KV_LORA])
    c = (cz * _rms(cz, MLA_KV_LORA)) * kvan_ref[...]
    lat_ref[...] = c
    kp = _mm(xn, win_ref[:, O_KPE:O_KPE + 2 * LANE])
    kpe_t = kp[:, :LANE] * cs + kp[:, LANE:] * sn
    kpe_ref[...] = kpe_t[:, :MLA_ROPE]
    kv = _mm(c.astype(BF16), wukv_ref[...])
    va_ref[...] = kv[:, MLA_HEADS * MLA_NOPE:].astype(BF16)
    pes = jnp.sum(kpe_t * kpe_t, axis=-1, keepdims=True)
    kg = kg_ref[...]
    for h in range(MLA_HEADS):
        kh = kv[:, h * MLA_NOPE:(h + 1) * MLA_NOPE]
        rr = lax.rsqrt((jnp.sum(kh * kh, axis=-1, keepdims=True) + pes) * (1.0 / MLA_QK) + EPS)
        kn_ref[h] = jnp.concatenate([kh * rr * kg[:, :LANE], kpe_t * rr * kg[:, LANE:]], axis=-1).astype(BF16)

    qb = _mm(xn, win_ref[:, O_QB:O_QB + WIDTH_B])
    m_scale = MOBA_DIM ** -0.5
    for h in range(MOBA_HEADS):
        t = qb[:, h * MOBA_DIM:(h + 1) * MOBA_DIM]
        t = (t * _rms(t, MOBA_DIM)) * mqn_ref[...]
        qbf_ref[:, h * MOBA_DIM:(h + 1) * MOBA_DIM] = t
        qb16_ref[h] = (t * m_scale).astype(BF16)
    kbz = _mm(xn, win_ref[:, O_KB:O_KB + MOBA_KV_HEADS * MOBA_DIM])
    kparts = []
    for g in range(MOBA_KV_HEADS):
        t = kbz[:, g * MOBA_DIM:(g + 1) * MOBA_DIM]
        kparts.append((t * _rms(t, MOBA_DIM)) * mkn_ref[...])
    kb = jnp.concatenate(kparts, axis=-1)
    kb_ref[...] = kb
    kb16_ref[...] = kb.astype(BF16)
    kmean_ref[0] = jnp.mean(kb, axis=0, keepdims=True)
    vb = _mm(xn, win_ref[:, O_VB:O_VB + MOBA_KV_HEADS * MOBA_DIM])
    vb_ref[...] = vb
    vb16_ref[...] = vb.astype(BF16)

    sga_ref[...] = _silu(_mm(xn, win_ref[:, O_GA:O_GA + WIDTH_A])).astype(BF16)
    sgb_ref[...] = _silu(_mm(xn, win_ref[:, O_GB:O_GB + WIDTH_B])).astype(BF16)


def _even_project(x, cs, sn, w, q_dtype):
    rows = x.shape[0]
    tm = min(MOBA_BLOCK, rows)
    assert rows % tm == 0
    nt = rows // tm
    row = lambda width: pl.BlockSpec((tm, width), lambda i: (i, 0))
    head = lambda width: pl.BlockSpec((MLA_HEADS, tm, width), lambda i: (0, i, 0))
    kvw = MOBA_KV_HEADS * MOBA_DIM
    out_shape = (
        jax.ShapeDtypeStruct((MLA_HEADS, rows, MLA_QK_PAD), q_dtype),
        jax.ShapeDtypeStruct((MLA_HEADS, rows, MLA_QK_PAD), BF16),
        jax.ShapeDtypeStruct((rows, WIDTH_A), BF16),
        jax.ShapeDtypeStruct((rows, MLA_KV_LORA), F32),
        jax.ShapeDtypeStruct((rows, MLA_ROPE), F32),
        jax.ShapeDtypeStruct((MOBA_HEADS, rows, MOBA_DIM), BF16),
        jax.ShapeDtypeStruct((rows, WIDTH_B), F32),
        jax.ShapeDtypeStruct((rows, kvw), F32),
        jax.ShapeDtypeStruct((rows, kvw), F32),
        jax.ShapeDtypeStruct((rows, kvw), BF16),
        jax.ShapeDtypeStruct((rows, kvw), BF16),
        jax.ShapeDtypeStruct((nt, 1, kvw), F32),
        jax.ShapeDtypeStruct((rows, WIDTH_A), BF16),
        jax.ShapeDtypeStruct((rows, WIDTH_B), BF16),
    )
    out_specs = (
        head(MLA_QK_PAD), head(MLA_QK_PAD), row(WIDTH_A), row(MLA_KV_LORA), row(MLA_ROPE),
        head(MOBA_DIM), row(WIDTH_B), row(kvw), row(kvw), row(kvw), row(kvw),
        pl.BlockSpec((1, 1, kvw), lambda i: (i, 0, 0)), row(WIDTH_A), row(WIDTH_B),
    )
    in_specs = [
        row(D_MODEL), row(LANE), row(LANE), _const_spec((1, D_MODEL)), _const_spec((D_MODEL, EVEN_COLS)),
        _const_spec((1, MLA_Q_LORA)), _const_spec((MLA_Q_LORA, MLA_HEADS * UQ_HEAD_COLS)),
        _const_spec((1, MLA_KV_LORA)), _const_spec((1, MLA_QK_PAD)), _const_spec((1, MLA_QK_PAD)),
        _const_spec((1, MOBA_DIM)), _const_spec((1, MOBA_DIM)),
        _const_spec((MLA_KV_LORA, MLA_HEADS * (MLA_NOPE + MLA_V))),
    ]
    return pl.pallas_call(
        _proj_kernel, grid=(nt,), in_specs=in_specs, out_specs=out_specs, out_shape=out_shape,
        compiler_params=_params(("arbitrary",)),
    )(x, cs, sn, w["norm"], w["w_in"], w["qa_norm"], w["w_uq"], w["kva_norm"], w["q_gain"], w["k_gain"],
      w["moba_qn"], w["moba_kn"], w["w_ukv"])


def _gate_kernel(q_ref, km_ref, b_ref, *, nb):
    i = pl.program_id(0)
    nbp = -(-nb // SUBLANE) * SUBLANE
    tq = q_ref.shape[0]
    jrow = lax.broadcasted_iota(jnp.int32, (nbp, tq), 0)
    valid = jrow < i
    for h in range(MOBA_HEADS):
        g = h // MOBA_REP
        km = km_ref[:nbp, g * MOBA_DIM:(g + 1) * MOBA_DIM]
        qh = q_ref[:, h * MOBA_DIM:(h + 1) * MOBA_DIM]
        gate = _mm_nt(km, qh, precision=lax.Precision.HIGHEST)
        gate = jnp.where(valid, gate, -jnp.inf)
        rank = jnp.zeros((nbp, tq), F32)
        for o in range(nb):
            other = gate[o:o + 1, :]
            ahead = (other > gate) | ((other == gate) & (o < jrow))
            rank = rank + jnp.where(ahead, 1.0, 0.0)
        keep = ((rank < MOBA_TOPK) & valid) | (jrow == i)
        bias_t = jnp.where(keep, 0.0, NEG)
        bias_t = jnp.concatenate([bias_t, jnp.zeros((LANE - nbp, tq), F32)], axis=0)
        b_ref[h] = bias_t.T.astype(BF16)


def _moba_gate(qbf, kmean, nb):
    rows = qbf.shape[0]
    assert nb <= LANE and rows == nb * MOBA_BLOCK
    kvw = MOBA_KV_HEADS * MOBA_DIM
    km = jnp.zeros((LANE, kvw), F32).at[:nb].set(kmean.reshape(nb, kvw))
    return pl.pallas_call(
        functools.partial(_gate_kernel, nb=nb), grid=(nb,),
        in_specs=[pl.BlockSpec((MOBA_BLOCK, WIDTH_B), lambda i: (i, 0)), _const_spec((LANE, kvw))],
        out_specs=pl.BlockSpec((MOBA_HEADS, MOBA_BLOCK, LANE), lambda i: (0, i, 0)),
        out_shape=jax.ShapeDtypeStruct((MOBA_HEADS, rows, LANE), BF16),
        compiler_params=_params(("arbitrary",)),
    )(qbf, km)


def _tri_pairs(n):
    ii, jj = [], []
    for i in range(n):
        for j in range(i + 1):
            ii.append(i)
            jj.append(j)
    return jnp.asarray(ii, jnp.int32), jnp.asarray(jj, jnp.int32)


def _rep(x, width):
    return x if width == LANE else jnp.concatenate([x] * (width // LANE), axis=-1)


def _flash_update(scores, values, scratch):
    probs, alphas = [], []
    for s, (m_scr, l_scr, _) in zip(scores, scratch):
        m_prev = m_scr[...]
        m_new = jnp.maximum(m_prev, jnp.max(s, axis=-1, keepdims=True))
        alpha = jnp.exp(m_prev - m_new)
        p = jnp.exp(s - _rep(m_new, s.shape[-1]))
        l_scr[...] = alpha * l_scr[...] + jnp.sum(p, axis=-1, keepdims=True)
        m_scr[...] = m_new
        probs.append(p.astype(BF16))
        alphas.append(alpha)
    for p, alpha, v, (_, _, acc_scr) in zip(probs, alphas, values, scratch):
        acc_scr[...] = _rep(alpha, acc_scr.shape[-1]) * acc_scr[...] + _mm(p, v)


def _flash_init(scratch):
    for m_scr, l_scr, acc_scr in scratch:
        m_scr[...] = jnp.full(m_scr.shape, NEG, F32)
        l_scr[...] = jnp.zeros(l_scr.shape, F32)
        acc_scr[...] = jnp.zeros(acc_scr.shape, F32)


MLA_HEADS_PER_STEP = 2


def _mla_prompt_kernel(ii_ref, jj_ref, q_ref, k_ref, v_ref, o_ref, m_scr, l_scr, acc_scr):
    t = pl.program_id(1)
    i = ii_ref[t]
    j = jj_ref[t]
    tq, tk = q_ref.shape[1], k_ref.shape[1]
    scratch = [(m_scr.at[a], l_scr.at[a], acc_scr.at[a]) for a in range(MLA_HEADS_PER_STEP)]

    @pl.when(j == 0)
    def _():
        _flash_init(scratch)

    ahead = lax.broadcasted_iota(jnp.int32, (tq, tk), 1) - lax.broadcasted_iota(jnp.int32, (tq, tk), 0)
    visible = ahead <= (i * tq - j * tk)
    scores = [jnp.where(visible, _mm_nt(q_ref[a], k_ref[a]), NEG) for a in range(MLA_HEADS_PER_STEP)]
    values = [v_ref[:, a * MLA_V:(a + 1) * MLA_V] for a in range(MLA_HEADS_PER_STEP)]
    _flash_update(scores, values, scratch)

    @pl.when(j == i)
    def _():
        for a in range(MLA_HEADS_PER_STEP):
            o_ref[:, a * MLA_V:(a + 1) * MLA_V] = acc_scr[a] / l_scr[a]


def _mla_prompt(qa, kn, va):
    rows = qa.shape[1]
    t = min(512, rows)
    assert rows % t == 0
    hp = MLA_HEADS_PER_STEP
    ii, jj = _tri_pairs(rows // t)
    grid_spec = pltpu.PrefetchScalarGridSpec(
        num_scalar_prefetch=2, grid=(MLA_HEADS // hp, ii.shape[0]),
        in_specs=[
            pl.BlockSpec((hp, t, MLA_QK_PAD), lambda h, s, ii, jj: (h, ii[s], 0)),
            pl.BlockSpec((hp, t, MLA_QK_PAD), lambda h, s, ii, jj: (h, jj[s], 0)),
            pl.BlockSpec((t, hp * MLA_V), lambda h, s, ii, jj: (jj[s], h)),
        ],
        out_specs=pl.BlockSpec((t, hp * MLA_V), lambda h, s, ii, jj: (ii[s], h)),
        scratch_shapes=[pltpu.VMEM((hp, t, LANE), F32), pltpu.VMEM((hp, t, LANE), F32),
                        pltpu.VMEM((hp, t, MLA_V), F32)],
    )
    return pl.pallas_call(
        _mla_prompt_kernel, grid_spec=grid_spec, out_shape=jax.ShapeDtypeStruct((rows, WIDTH_A), F32),
        compiler_params=_params(("arbitrary", "arbitrary")),
    )(ii, jj, qa, kn, va)


def _moba_prompt_kernel(ii_ref, jj_ref, q_ref, b_ref, k_ref, v_ref, o_ref, m_scr, l_scr, acc_scr):
    t = pl.program_id(0)
    i = ii_ref[t]
    j = jj_ref[t]
    blk = MOBA_BLOCK
    nrow = MOBA_REP * blk
    scratch = [(m_scr.at[g], l_scr.at[g], acc_scr.at[g]) for g in range(MOBA_KV_HEADS)]

    @pl.when(j == 0)
    def _():
        _flash_init(scratch)

    onehot = jnp.where(lax.broadcasted_iota(jnp.int32, (blk, LANE), 1) == j, 1.0, 0.0).astype(BF16)
    qpos = lax.broadcasted_iota(jnp.int32, (MOBA_REP, blk, blk), 1).reshape(nrow, blk)
    kpos = lax.broadcasted_iota(jnp.int32, (nrow, blk), 1)
    visible = (j < i) | (kpos <= qpos)
    scores, values = [], []
    for g in range(MOBA_KV_HEADS):
        heads = slice(g * MOBA_REP, (g + 1) * MOBA_REP)
        q_aug = jnp.concatenate([q_ref[heads].reshape(nrow, MOBA_DIM), b_ref[heads].reshape(nrow, LANE)], axis=-1)
        k_aug = jnp.concatenate([k_ref[:, g * MOBA_DIM:(g + 1) * MOBA_DIM], onehot], axis=-1)
        scores.append(jnp.where(visible, _mm_nt(q_aug, k_aug), NEG))
        values.append(v_ref[:, g * MOBA_DIM:(g + 1) * MOBA_DIM])
    _flash_update(scores, values, scratch)

    @pl.when(j == i)
    def _():
        for g in range(MOBA_KV_HEADS):
            o = acc_scr[g] / l_scr[g]
            for r in range(MOBA_REP):
                h = g * MOBA_REP + r
                o_ref[:, h * MOBA_DIM:(h + 1) * MOBA_DIM] = o[r * blk:(r + 1) * blk]


def _moba_prompt(qb16, bias, kb16, vb16):
    rows = qb16.shape[1]
    nb = rows // MOBA_BLOCK
    ii, jj = _tri_pairs(nb)
    nrow = MOBA_REP * MOBA_BLOCK
    kvw = MOBA_KV_HEADS * MOBA_DIM
    grid_spec = pltpu.PrefetchScalarGridSpec(
        num_scalar_prefetch=2, grid=(ii.shape[0],),
        in_specs=[
            pl.BlockSpec((MOBA_HEADS, MOBA_BLOCK, MOBA_DIM), lambda s, ii, jj: (0, ii[s], 0)),
            pl.BlockSpec((MOBA_HEADS, MOBA_BLOCK, LANE), lambda s, ii, jj: (0, ii[s], 0)),
            pl.BlockSpec((MOBA_BLOCK, kvw), lambda s, ii, jj: (jj[s], 0)),
            pl.BlockSpec((MOBA_BLOCK, kvw), lambda s, ii, jj: (jj[s], 0)),
        ],
        out_specs=pl.BlockSpec((MOBA_BLOCK, WIDTH_B), lambda s, ii, jj: (ii[s], 0)),
        scratch_shapes=[pltpu.VMEM((MOBA_KV_HEADS, nrow, LANE), F32), pltpu.VMEM((MOBA_KV_HEADS, nrow, LANE), F32),
                        pltpu.VMEM((MOBA_KV_HEADS, nrow, MOBA_DIM), F32)],
    )
    return pl.pallas_call(
        _moba_prompt_kernel, grid_spec=grid_spec, out_shape=jax.ShapeDtypeStruct((rows, WIDTH_B), F32),
        compiler_params=_params(("arbitrary",)),
    )(ii, jj, qb16, bias, kb16, vb16)


PAGES_PER_STEP = 16
SUB_KEYS = 512


def _paged_copies(pt_ref, srcs, dsts, sem, b, j, slot):
    copies = []
    for p in range(PAGES_PER_STEP):
        page = pt_ref[b, j * PAGES_PER_STEP + p]
        for a, (src, dst) in enumerate(zip(srcs, dsts)):
            copies.append(pltpu.make_async_copy(src.at[page], dst(slot, p), sem.at[a, slot]))
    return copies


def _paged_pipeline(pt_ref, srcs, dsts, sem):
    b = pl.program_id(0)
    j = pl.program_id(1)
    nj = pl.num_programs(1)
    step = b * nj + j
    slot = step % 2

    @pl.when(step == 0)
    def _():
        for cp in _paged_copies(pt_ref, srcs, dsts, sem, b, j, slot):
            cp.start()

    @pl.when(step + 1 < pl.num_programs(0) * nj)
    def _():
        last = j + 1 == nj
        nb_ = jnp.where(last, b + 1, b)
        nj_ = jnp.where(last, 0, j + 1)
        for cp in _paged_copies(pt_ref, srcs, dsts, sem, nb_, nj_, 1 - slot):
            cp.start()

    for cp in _paged_copies(pt_ref, srcs, dsts, sem, b, j, slot):
        cp.wait()
    return slot


def _mla_paged_kernel(pt_ref, qbd_ref, qpe_ref, wukt_ref, wuv_ref, cnew_ref, knew_ref, lat_hbm, kpe_hbm,
                      o_ref, lat_buf, kpe_buf, sem, lhs_scr, m_scr, l_scr, ctx_scr, *, page_size):
    b = pl.program_id(0)
    j = pl.program_id(1)
    nj = pl.num_programs(1)
    nq = qbd_ref.shape[1]
    nhd = MLA_HEADS * MLA_NOPE
    scratch = [(m_scr, l_scr, ctx_scr)]

    @pl.when((b == 0) & (j == 0))
    def _():
        lhs_scr[:nhd, :] = wukt_ref[...]

    @pl.when(j == 0)
    def _():
        lhs_scr[nhd:, :] = _mm(qbd_ref[0], wukt_ref[...]).astype(BF16)
        _flash_init(scratch)

    slot = _paged_pipeline(
        pt_ref, (lat_hbm, kpe_hbm),
        (lambda s, p: lat_buf.at[s, pl.ds(p * page_size, page_size)],
         lambda s, p: kpe_buf.at[s, :, pl.ds(p * page_size, page_size)]), sem)

    def project(c):
        cb = c.astype(BF16)
        return cb, _mm_nt(lhs_scr[...], cb)

    def attend(cb, res, kpe_t, mask):
        n = cb.shape[0]
        kn = res[:nhd]
        n2 = jnp.sum((kn * kn).reshape(MLA_HEADS, MLA_NOPE, n), axis=1)
        pe2 = jnp.sum(kpe_t * kpe_t, axis=0, keepdims=True)
        spe = _mm(qpe_ref[0], kpe_t.astype(BF16))
        r = lax.rsqrt((n2 + pe2) * (1.0 / MLA_QK) + EPS)
        s = ((res[nhd:] + spe).reshape(nq // MLA_HEADS, MLA_HEADS, n) * r[None]).reshape(nq, n)
        if mask is not None:
            s = jnp.where(mask, s, NEG)
        _flash_update([s], [cb], scratch)

    keys = PAGES_PER_STEP * page_size
    sub = min(SUB_KEYS, keys)
    pending = None
    for u in range(keys // sub):
        cur = project(lat_buf[slot, u * sub:(u + 1) * sub, :]) + (kpe_buf[slot, :, u * sub:(u + 1) * sub],)
        if pending is not None:
            attend(*pending, None)
        pending = cur
    attend(*pending, None)

    @pl.when(j == nj - 1)
    def _():
        nnew = cnew_ref.shape[1]
        qidx = lax.broadcasted_iota(jnp.int32, (nq // MLA_HEADS, MLA_HEADS, nnew), 0).reshape(nq, nnew)
        kidx = lax.broadcasted_iota(jnp.int32, (nq, nnew), 1)
        cb, res = project(cnew_ref[0])
        attend(cb, res, knew_ref[0], kidx <= qidx)
        ctx = (ctx_scr[...] / _rep(l_scr[...], MLA_KV_LORA)).astype(BF16)
        full = _mm(ctx, wuv_ref[...])
        hrow = lax.broadcasted_iota(jnp.int32, (nq // MLA_HEADS, MLA_HEADS, MLA_V), 1).reshape(nq, MLA_V)
        out = jnp.zeros((nq, MLA_V), F32)
        for h in range(MLA_HEADS):
            out = out + jnp.where(hrow == h, full[:, h * MLA_V:(h + 1) * MLA_V], 0.0)
        o_ref[0] = out


def _mla_paged(page_table, qbd, qpe, wukt, wuv, cnew, knew_t, cache_lat, cache_kpe_t):
    ns, n_pages = page_table.shape
    page_size = cache_lat.shape[1]
    assert n_pages % PAGES_PER_STEP == 0
    nq = qbd.shape[1]
    keys = PAGES_PER_STEP * page_size
    nhd = MLA_HEADS * MLA_NOPE
    per_seq = lambda shape: pl.BlockSpec((1,) + shape, lambda b, j, pt: (b, 0, 0))
    grid_spec = pltpu.PrefetchScalarGridSpec(
        num_scalar_prefetch=1, grid=(ns, n_pages // PAGES_PER_STEP),
        in_specs=[
            per_seq((nq, nhd)), per_seq((nq, MLA_ROPE)),
            _const_spec((nhd, MLA_KV_LORA)), _const_spec((MLA_KV_LORA, MLA_HEADS * MLA_V)),
            per_seq(cnew.shape[1:]), per_seq(knew_t.shape[1:]),
            pl.BlockSpec(memory_space=pl.ANY), pl.BlockSpec(memory_space=pl.ANY),
        ],
        out_specs=per_seq((nq, MLA_V)),
        scratch_shapes=[
            pltpu.VMEM((2, keys, MLA_KV_LORA), F32), pltpu.VMEM((2, MLA_ROPE, keys), F32),
            pltpu.SemaphoreType.DMA((2, 2)),
            pltpu.VMEM((nhd + nq, MLA_KV_LORA), BF16),
            pltpu.VMEM((nq, LANE), F32), pltpu.VMEM((nq, LANE), F32), pltpu.VMEM((nq, MLA_KV_LORA), F32),
        ],
    )
    return pl.pallas_call(
        functools.partial(_mla_paged_kernel, page_size=page_size), grid_spec=grid_spec,
        out_shape=jax.ShapeDtypeStruct((ns, nq, MLA_V), F32),
        compiler_params=_params(("arbitrary", "arbitrary")),
    )(page_table, qbd, qpe, wukt, wuv, cnew, knew_t, cache_lat, cache_kpe_t)


def _moba_paged_kernel(pt_ref, q16_ref, qf_ref, knew_ref, vnew_ref, k_hbm, v_hbm, o_ref,
                       k_buf, v_buf, sem, m_s, l_s, g_s, o_s, *, page_size, nb):
    j = pl.program_id(1)
    nj = pl.num_programs(1)
    nq = q16_ref.shape[1]
    blocks_per_step = PAGES_PER_STEP * page_size // MOBA_BLOCK
    page_rows = MOBA_KV_HEADS * page_size

    window = lambda buf: (lambda s, p: buf.at[s, pl.ds(p * page_rows, page_rows)])
    slot = _paged_pipeline(pt_ref, (k_hbm, v_hbm), (window(k_buf), window(v_buf)), sem)

    def block(buf, u):
        base = u * MOBA_KV_HEADS * MOBA_BLOCK
        return jnp.concatenate(
            [buf[slot, pl.ds(base + g, MOBA_BLOCK, stride=MOBA_KV_HEADS), :] for g in range(MOBA_KV_HEADS)], axis=-1)

    q16 = q16_ref[0]
    qf = qf_ref[0]
    wide = (nq, LANE)
    k16 = []
    for u in range(blocks_per_step):
        kf = block(k_buf, u)
        kmean = jnp.mean(kf, axis=0, keepdims=True)
        g_s[j * blocks_per_step + u] = jnp.broadcast_to(jnp.sum(qf * kmean, axis=-1, keepdims=True), wide)
        k16.append(kf.astype(BF16))
    s_all = _mm_nt(q16, jnp.concatenate(k16, axis=0))
    probs = []
    for u in range(blocks_per_step):
        jb = j * blocks_per_step + u
        s = s_all[:, u * MOBA_BLOCK:(u + 1) * MOBA_BLOCK]
        m = jnp.max(s, axis=-1, keepdims=True)
        p = jnp.exp(s - m)
        m_s[jb] = jnp.broadcast_to(m, wide)
        l_s[jb] = jnp.broadcast_to(jnp.sum(p, axis=-1, keepdims=True), wide)
        probs.append(p.astype(BF16))
    for u in range(blocks_per_step):
        o_s[j * blocks_per_step + u] = _mm(probs[u], block(v_buf, u).astype(BF16))

    @pl.when(j == nj - 1)
    def _():
        lane = lax.broadcasted_iota(jnp.int32, wide, 1)

        def gather(jb, carry):
            gm, mm, lm = carry
            hit = lane == jb
            return jnp.where(hit, g_s[jb], gm), jnp.where(hit, m_s[jb], mm), jnp.where(hit, l_s[jb], lm)
        gm, mm, lm = lax.fori_loop(
            0, nb, gather, (jnp.full(wide, -jnp.inf, F32), jnp.full(wide, NEG, F32), jnp.zeros(wide, F32)))

        def count(o, rank):
            other = g_s[o]
            ahead = (other > gm) | ((other == gm) & (o < lane))
            return rank + jnp.where(ahead, 1.0, 0.0)
        rank = lax.fori_loop(0, nb, count, jnp.zeros(wide, F32))
        keep = (rank < MOBA_TOPK) & (lane < nb)

        nnew = knew_ref.shape[1]
        qidx = lax.broadcasted_iota(jnp.int32, (nq // MOBA_HEADS, MOBA_HEADS, nnew), 0).reshape(nq, nnew)
        kidx = lax.broadcasted_iota(jnp.int32, (nq, nnew), 1)
        s_own = jnp.where(kidx <= qidx, _mm_nt(q16, knew_ref[0].astype(BF16)), NEG)
        m_own = jnp.max(s_own, axis=-1, keepdims=True)
        m_tot = jnp.maximum(jnp.max(jnp.where(keep, mm, NEG), axis=-1, keepdims=True), m_own)
        p_own = jnp.exp(s_own - m_tot)
        wgt = jnp.exp(jnp.where(keep, mm - m_tot, NEG))
        l_tot = jnp.sum(wgt * lm, axis=-1, keepdims=True) + jnp.sum(p_own, axis=-1, keepdims=True)
        acc0 = _mm(p_own.astype(BF16), vnew_ref[0].astype(BF16))

        def combine(jb, acc):
            w_col = jnp.sum(jnp.where(lane == jb, wgt, 0.0), axis=-1, keepdims=True)
            return acc + w_col * o_s[jb]
        acc = lax.fori_loop(0, nb, combine, acc0) / l_tot
        grow = lax.broadcasted_iota(jnp.int32, (nq // MOBA_HEADS, MOBA_HEADS, MOBA_DIM), 1).reshape(nq, MOBA_DIM)
        o_ref[0] = jnp.where(grow < MOBA_REP, acc[:, :MOBA_DIM], acc[:, MOBA_DIM:])


def _moba_paged(page_table, q16, qf, knew, vnew, cache_k, cache_v, page_size):
    ns, n_pages = page_table.shape
    past = n_pages * page_size
    assert n_pages % PAGES_PER_STEP == 0 and past % MOBA_BLOCK == 0
    assert (PAGES_PER_STEP * page_size) % MOBA_BLOCK == 0
    nb = past // MOBA_BLOCK
    assert nb <= LANE
    nq = q16.shape[1]
    kvw = MOBA_KV_HEADS * MOBA_DIM
    rows = PAGES_PER_STEP * page_size * MOBA_KV_HEADS
    per_seq = lambda shape: pl.BlockSpec((1,) + shape, lambda b, j, pt: (b, 0, 0))
    grid_spec = pltpu.PrefetchScalarGridSpec(
        num_scalar_prefetch=1, grid=(ns, n_pages // PAGES_PER_STEP),
        in_specs=[
            per_seq((nq, kvw)), per_seq((nq, kvw)), per_seq(knew.shape[1:]), per_seq(vnew.shape[1:]),
            pl.BlockSpec(memory_space=pl.ANY), pl.BlockSpec(memory_space=pl.ANY),
        ],
        out_specs=per_seq((nq, MOBA_DIM)),
        scratch_shapes=[
            pltpu.VMEM((2, rows, MOBA_DIM), F32), pltpu.VMEM((2, rows, MOBA_DIM), F32),
            pltpu.SemaphoreType.DMA((2, 2)),
            pltpu.VMEM((nb, nq, LANE), F32), pltpu.VMEM((nb, nq, LANE), F32), pltpu.VMEM((nb, nq, LANE), F32),
            pltpu.VMEM((nb, nq, kvw), F32),
        ],
    )
    return pl.pallas_call(
        functools.partial(_moba_paged_kernel, page_size=page_size, nb=nb), grid_spec=grid_spec,
        out_shape=jax.ShapeDtypeStruct((ns, nq, MOBA_DIM), F32),
        compiler_params=_params(("arbitrary", "arbitrary")),
    )(page_table, q16, qf, knew, vnew, cache_k, cache_v)


def _merge_kernel(x_ref, oa_ref, ob_ref, sga_ref, sgb_ref, w_ref, o_ref):
    mix = jnp.concatenate([(oa_ref[...] * sga_ref[...].astype(F32)).astype(BF16),
                           (ob_ref[...] * sgb_ref[...].astype(F32)).astype(BF16)], axis=-1)
    o_ref[...] = x_ref[...] + _mm(mix, w_ref[...])


def _even_merge(x, oa, ob, sga, sgb, w_out):
    rows = x.shape[0]
    tm = min(256, rows)
    assert rows % tm == 0
    row = lambda width: pl.BlockSpec((tm, width), lambda i: (i, 0))
    return pl.pallas_call(
        _merge_kernel, grid=(rows // tm,),
        in_specs=[row(D_MODEL), row(WIDTH_A), row(WIDTH_B), row(WIDTH_A), row(WIDTH_B),
                  _const_spec((WIDTH_A + WIDTH_B, D_MODEL))],
        out_specs=row(D_MODEL), out_shape=jax.ShapeDtypeStruct((rows, D_MODEL), F32),
        compiler_params=_params(("arbitrary",)),
    )(x, oa, ob, sga, sgb, w_out)


def _conv_kernel(x_ref, norm_ref, wb_ref, wc_ref, wh_ref, wg_ref, cw_ref, wo_ref, prev_ref,
                 o_ref, st_ref, xn_scr, acc_scr, carry_scr, *, step_major_seqs):
    i = pl.program_id(0)
    n = pl.program_id(1)
    tm = x_ref.shape[0]

    @pl.when(n == 0)
    def _():
        x = x_ref[...]
        xn_scr[...] = ((x * _rms(x, D_MODEL)) * norm_ref[...]).astype(BF16)
        acc_scr[...] = x

    xn = xn_scr[...]
    ch = _mm(xn, wc_ref[...]) * _mm(xn, wh_ref[...])
    cw = cw_ref[...]
    if step_major_seqs:
        ns = step_major_seqs
        full = jnp.concatenate([prev_ref[...], ch], axis=0)
        conv = cw[0:1] * full[0:tm] + cw[1:2] * full[ns:ns + tm] + cw[2:3] * full[2 * ns:2 * ns + tm]
        st_ref[...] = ch[tm - 2 * ns:]
    else:
        @pl.when(i == 0)
        def _():
            carry_scr[n] = jnp.zeros(carry_scr.shape[1:], F32)

        prev = carry_scr[n]
        rowi = lax.broadcasted_iota(jnp.int32, ch.shape, 0)
        m1 = jnp.where(rowi == 0, prev[SUBLANE - 1:SUBLANE], pltpu.roll(ch, 1, 0))
        m2 = jnp.where(rowi == 0, prev[SUBLANE - 2:SUBLANE - 1],
                       jnp.where(rowi == 1, prev[SUBLANE - 1:SUBLANE], pltpu.roll(ch, 2, 0)))
        conv = cw[0:1] * m2 + cw[1:2] * m1 + cw[2:3] * ch
        carry_scr[n] = ch[tm - SUBLANE:]
        st_ref[...] = ch[tm - SUBLANE:]
    y = (_mm(xn, wb_ref[...]) * conv * _silu(_mm(xn, wg_ref[...]))).astype(BF16)
    acc_scr[...] += _mm(y, wo_ref[...])

    @pl.when(n == pl.num_programs(1) - 1)
    def _():
        o_ref[...] = acc_scr[...]


def _conv_layer(x, prev, w, step_major_seqs):
    rows = x.shape[0]
    tm = rows if step_major_seqs else min(512, rows)
    tn = 512
    assert rows % tm == 0
    ncol = CONV_DIM // tn
    st_rows = prev.shape[0] if step_major_seqs else SUBLANE
    wcol = lambda k: pl.BlockSpec((D_MODEL, tn), lambda i, n, k=k: (0, n + k * ncol))
    return pl.pallas_call(
        functools.partial(_conv_kernel, step_major_seqs=step_major_seqs), grid=(rows // tm, ncol),
        in_specs=[
            pl.BlockSpec((tm, D_MODEL), lambda i, n: (i, 0)), _const_spec((1, D_MODEL)),
            wcol(0), wcol(1), wcol(2), wcol(3),
            pl.BlockSpec((CONV_WIDTH, tn), lambda i, n: (0, n)),
            pl.BlockSpec((tn, D_MODEL), lambda i, n: (n, 0)),
            pl.BlockSpec((prev.shape[0], tn), lambda i, n: (0, n)),
        ],
        out_specs=(pl.BlockSpec((tm, D_MODEL), lambda i, n: (i, 0)),
                   pl.BlockSpec((st_rows, tn), lambda i, n: (i, n))),
        out_shape=(jax.ShapeDtypeStruct((rows, D_MODEL), F32),
                   jax.ShapeDtypeStruct((rows // tm * st_rows, CONV_DIM), F32)),
        scratch_shapes=[pltpu.VMEM((tm, D_MODEL), BF16), pltpu.VMEM((tm, D_MODEL), F32),
                        pltpu.VMEM((ncol, SUBLANE, tn), F32)],
        compiler_params=_params(("arbitrary", "arbitrary")),
    )(x, w["norm"], w["w_in"], w["w_in"], w["w_in"], w["w_in"], w["conv_w"], w["w_out"], prev)


def _rot_cols(w):
    half = MLA_ROPE // 2
    return jnp.concatenate([-w[..., half:], w[..., :half]], axis=-1)


def _pad_cols(w, width):
    return jnp.pad(w, ((0, 0), (0, width - w.shape[-1])))


def _gain_pad(g):
    full = jnp.concatenate([g[:MLA_NOPE], g[MLA_NOPE:], g[MLA_NOPE:]])
    return jnp.pad(full, (0, MLA_QK_PAD - MLA_QK)).reshape(1, MLA_QK_PAD)


def _even_weights(norm, w_in, qa_norm, w_uq, kva_norm, w_uk, w_uv, q_norm, k_norm, moba_qn, moba_kn):
    o = np.cumsum((0, MLA_Q_LORA, MLA_KV_LORA, MLA_ROPE, WIDTH_B, MOBA_KV_HEADS * MOBA_DIM,
                   MOBA_KV_HEADS * MOBA_DIM, WIDTH_A, WIDTH_B))
    kpe_w = w_in[:, o[2]:o[3]]
    w_in2 = jnp.concatenate([
        w_in[:, o[0]:o[2]], _pad_cols(kpe_w, LANE), _pad_cols(_rot_cols(kpe_w), LANE), w_in[:, o[3]:],
    ], axis=-1).astype(BF16)
    uq = w_uq.reshape(MLA_Q_LORA, MLA_HEADS, MLA_QK)
    pe_w = uq[..., MLA_NOPE:]
    pad = jnp.zeros((MLA_Q_LORA, MLA_HEADS, LANE - MLA_ROPE), F32)
    uq2 = jnp.concatenate([uq[..., :MLA_NOPE], pe_w, pad, _rot_cols(pe_w), pad], axis=-1)
    return {
        "norm": norm.reshape(1, D_MODEL), "w_in": w_in2, "qa_norm": qa_norm.reshape(1, MLA_Q_LORA),
        "w_uq": uq2.reshape(MLA_Q_LORA, MLA_HEADS * UQ_HEAD_COLS).astype(BF16),
        "kva_norm": kva_norm.reshape(1, MLA_KV_LORA), "q_gain": _gain_pad(q_norm), "k_gain": _gain_pad(k_norm),
        "moba_qn": moba_qn.reshape(1, MOBA_DIM), "moba_kn": moba_kn.reshape(1, MOBA_DIM),
        "w_ukv": jnp.concatenate([w_uk, w_uv], axis=-1).astype(BF16),
    }


def _rope_tables(pos):
    half = MLA_ROPE // 2
    inv = ROPE_THETA ** (-jnp.arange(half, dtype=F32) / half)
    ang = pos.astype(F32)[:, None] * inv
    cos, sin = jnp.cos(ang), jnp.sin(ang)
    pad = jnp.zeros((pos.shape[0], LANE - MLA_ROPE), F32)
    return jnp.concatenate([cos, cos, pad], axis=-1), jnp.concatenate([sin, sin, pad], axis=-1)


def _block_diag_rows(x, nblk):
    eye = jnp.eye(nblk, dtype=x.dtype)
    out = x[..., :, None, :] * eye[:, :, None]
    return out.reshape(x.shape[:-1] + (nblk * x.shape[-1],))


def kernel(x_prompt, x_sample, cache_mla_latent, cache_mla_kpe, cache_moba_k, cache_moba_v, state_conv, page_table, norm_even, w_in_even, mla_qa_norm, mla_w_uq, mla_kva_norm, mla_w_uk, mla_w_uv, mla_q_norm, mla_k_norm, moba_q_norm, moba_k_norm, w_out_even, norm_odd, w_in_odd, conv_w, w_out_odd):
    assert x_prompt.shape[0] == 1 and norm_even.shape[0] == 1 and norm_odd.shape[0] == 1
    seq = x_prompt.shape[1]
    ns, steps = x_sample.shape[:2]
    n_pages = page_table.shape[1]
    page_size = cache_mla_latent.shape[2]
    past = n_pages * page_size
    assert seq % MOBA_BLOCK == 0 and steps <= SUBLANE
    kvw = MOBA_KV_HEADS * MOBA_DIM

    we = _even_weights(norm_even[0], w_in_even[0], mla_qa_norm[0], mla_w_uq[0], mla_kva_norm[0], mla_w_uk[0],
                       mla_w_uv[0], mla_q_norm[0], mla_k_norm[0], moba_q_norm[0], moba_k_norm[0])
    w_out_e = w_out_even[0].astype(BF16)
    wo = {"norm": norm_odd[0].reshape(1, D_MODEL), "w_in": w_in_odd[0].astype(BF16), "conv_w": conv_w[0],
          "w_out": w_out_odd[0].astype(BF16)}

    xp = x_prompt[0]
    cs, sn = _rope_tables(jnp.arange(seq))
    (qa, kn, va, lat_p, kpe_p, qb16, qbf, kb_p, vb_p, kb16, vb16, kmean, sga, sgb) = _even_project(xp, cs, sn, we, BF16)
    oa = _mla_prompt(qa, kn, va)
    bias = _moba_gate(qbf, kmean, seq // MOBA_BLOCK)
    ob = _moba_prompt(qb16, bias, kb16, vb16)
    y1p = _even_merge(xp, oa, ob, sga, sgb, w_out_e)
    y2p, st_p = _conv_layer(y1p, jnp.zeros((SUBLANE, CONV_DIM), F32), wo, 0)

    rows_s = steps * ns
    xs = x_sample.transpose(1, 0, 2).reshape(rows_s, D_MODEL)
    cs_s, sn_s = _rope_tables(past + jnp.repeat(jnp.arange(steps), ns))
    (qa_s, _, _, lat_s, kpe_s, qb16_s, qbf_s, kb_s, vb_s, _, _, _, sga_s, sgb_s) = _even_project(xs, cs_s, sn_s, we, F32)

    def per_seq(a):
        return a.reshape(steps, ns, a.shape[-1]).transpose(1, 0, 2)

    def pad_new(a):
        return jnp.pad(a, ((0, 0), (0, LANE - steps), (0, 0)))

    nq = steps * MLA_HEADS
    qk = qa_s.reshape(MLA_HEADS, steps, ns, MLA_QK_PAD).transpose(2, 1, 0, 3) * we["k_gain"][0]
    qbd = _block_diag_rows(qk[..., :MLA_NOPE], MLA_HEADS).reshape(ns, nq, MLA_HEADS * MLA_NOPE).astype(BF16)
    qpe = qk[..., MLA_NOPE:MLA_QK].reshape(ns, nq, MLA_ROPE).astype(BF16)
    oa_s = _mla_paged(page_table, qbd, qpe, mla_w_uk[0].T.astype(BF16), mla_w_uv[0].astype(BF16),
                      pad_new(per_seq(lat_s)), pad_new(per_seq(kpe_s)).transpose(0, 2, 1),
                      cache_mla_latent[0], cache_mla_kpe[0].transpose(0, 2, 1))

    def group_diag(a):
        g = a.reshape(ns, steps, MOBA_KV_HEADS, MOBA_REP, MOBA_DIM).transpose(0, 1, 3, 2, 4)
        return _block_diag_rows(g, MOBA_KV_HEADS).transpose(0, 1, 3, 2, 4).reshape(ns, nq, kvw)

    q16_s = group_diag(qb16_s.reshape(MOBA_HEADS, steps, ns, MOBA_DIM).transpose(2, 1, 0, 3))
    qf_s = group_diag(per_seq(qbf_s).reshape(ns, steps, MOBA_HEADS, MOBA_DIM))
    ob_s = _moba_paged(page_table, q16_s, qf_s, pad_new(per_seq(kb_s)), pad_new(per_seq(vb_s)),
                       cache_moba_k[0].reshape(-1, page_size * MOBA_KV_HEADS, MOBA_DIM),
                       cache_moba_v[0].reshape(-1, page_size * MOBA_KV_HEADS, MOBA_DIM), page_size)

    def step_major(o):
        return o.reshape(ns, steps, -1).transpose(1, 0, 2).reshape(rows_s, -1)

    y1s = _even_merge(xs, step_major(oa_s), step_major(ob_s), sga_s, sgb_s, w_out_e)
    prev_s = state_conv[0].transpose(1, 0, 2).reshape((CONV_WIDTH - 1) * ns, CONV_DIM)
    y2s, st_s = _conv_layer(y1s, prev_s, wo, ns)

    return (
        y2p[None], per_seq(y2s),
        lat_p[None, None], kpe_p[None, None],
        kb_p.reshape(1, 1, seq, MOBA_KV_HEADS, MOBA_DIM), vb_p.reshape(1, 1, seq, MOBA_KV_HEADS, MOBA_DIM),
        st_p[st_p.shape[0] - (CONV_WIDTH - 1):][None, None],
        per_seq(lat_s)[None], per_seq(kpe_s)[None],
        per_seq(kb_s).reshape(1, ns, steps, MOBA_KV_HEADS, MOBA_DIM),
        per_seq(vb_s).reshape(1, ns, steps, MOBA_KV_HEADS, MOBA_DIM),
        st_s.reshape(CONV_WIDTH - 1, ns, CONV_DIM).transpose(1, 0, 2)[None],
    )
```

```python
import functools

import jax
import jax.numpy as jnp
import numpy as np
from jax import lax
from jax.experimental import pallas as pl
from jax.experimental.pallas import tpu as pltpu

F32 = jnp.float32
BF16 = jnp.bfloat16

D_MODEL = 2048
MLA_HEADS = 8
MLA_NOPE = 128
MLA_ROPE = 64
MLA_QK = MLA_NOPE + MLA_ROPE
MLA_QK_PAD = 256
MLA_V = 128
MLA_Q_LORA = 512
MLA_KV_LORA = 512
ROPE_THETA = 10000.0
MOBA_HEADS = 8
MOBA_KV_HEADS = 2
MOBA_REP = MOBA_HEADS // MOBA_KV_HEADS
MOBA_DIM = 128
MOBA_BLOCK = 256
MOBA_TOPK = 3
CONV_DIM = D_MODEL
CONV_WIDTH = 3
WIDTH_A = MLA_HEADS * MLA_V
WIDTH_B = MOBA_HEADS * MOBA_DIM
EPS = 1e-6
NEG = -1e30
LANE = 128
SUBLANE = 8
VMEM_LIMIT = 56 * 1024 * 1024

O_QLAT = 0
O_C = O_QLAT + MLA_Q_LORA
O_KPE = O_C + MLA_KV_LORA
O_QB = O_KPE + 2 * LANE
O_KB = O_QB + WIDTH_B
O_VB = O_KB + MOBA_KV_HEADS * MOBA_DIM
O_GA = O_VB + MOBA_KV_HEADS * MOBA_DIM
O_GB = O_GA + WIDTH_A
EVEN_COLS = O_GB + WIDTH_B
UQ_HEAD_COLS = 3 * LANE


def _mm(a, b):
    return jnp.dot(a, b, preferred_element_type=F32)


def _mm_nt(a, b, precision=None):
    return lax.dot_general(a, b, (((1,), (1,)), ((), ())), preferred_element_type=F32, precision=precision)


def _const_spec(shape):
    return pl.BlockSpec(shape, lambda *_: (0,) * len(shape), pipeline_mode=pl.Buffered(1))


def _params(sem):
    return pltpu.CompilerParams(dimension_semantics=sem, vmem_limit_bytes=VMEM_LIMIT)


def _rms(x, width):
    return lax.rsqrt(jnp.sum(x * x, axis=-1, keepdims=True) * (1.0 / width) + EPS)


def _silu(x):
    return x * jax.nn.sigmoid(x)


def _proj_kernel(x_ref, cs_ref, sn_ref, norm_ref, win_ref, qan_ref, wuq_ref, kvan_ref, qg_ref, kg_ref,
                 mqn_ref, mkn_ref, wukv_ref,
                 qa_ref, kn_ref, va_ref, lat_ref, kpe_ref, qb16_ref, qbf_ref, kb_ref, vb_ref, kb16_ref,
                 vb16_ref, kmean_ref, sga_ref, sgb_ref):
    x = x_ref[...]
    xn = ((x * _rms(x, D_MODEL)) * norm_ref[...]).astype(BF16)
    cs = cs_ref[...]
    sn = sn_ref[...]

    ql = _mm(xn, win_ref[:, O_QLAT:O_QLAT + MLA_Q_LORA])
    qln = ((ql * _rms(ql, MLA_Q_LORA)) * qan_ref[...]).astype(BF16)
    qh_all = _mm(qln, wuq_ref[...])
    qg = qg_ref[...]
    q_scale = MLA_QK ** -0.5
    for h in range(MLA_HEADS):
        base = h * UQ_HEAD_COLS
        nope = qh_all[:, base:base + LANE]
        pe = qh_all[:, base + LANE:base + 2 * LANE] * cs + qh_all[:, base + 2 * LANE:base + 3 * LANE] * sn
        ssq = jnp.sum(nope * nope, axis=-1, keepdims=True) + jnp.sum(pe * pe, axis=-1, keepdims=True)
        rr = lax.rsqrt(ssq * (1.0 / MLA_QK) + EPS) * q_scale
        qa_ref[h] = jnp.concatenate([nope * rr * qg[:, :LANE], pe * rr * qg[:, LANE:]], axis=-1).astype(qa_ref.dtype)

    cz = _mm(xn, win_ref[:, O_C:O_C + MLA_KV_LORA])
    c = (cz * _rms(cz, MLA_KV_LORA)) * kvan_ref[...]
    lat_ref[...] = c
    kp = _mm(xn, win_ref[:, O_KPE:O_KPE + 2 * LANE])
    kpe_t = kp[:, :LANE] * cs + kp[:, LANE:] * sn
    kpe_ref[...] = kpe_t[:, :MLA_ROPE]
    kv = _mm(c.astype(BF16), wukv_ref[...])
    va_ref[...] = kv[:, MLA_HEADS * MLA_NOPE:].astype(BF16)
    pes = jnp.sum(kpe_t * kpe_t, axis=-1, keepdims=True)
    kg = kg_ref[...]
    for h in range(MLA_HEADS):
        kh = kv[:, h * MLA_NOPE:(h + 1) * MLA_NOPE]
        rr = lax.rsqrt((jnp.sum(kh * kh, axis=-1, keepdims=True) + pes) * (1.0 / MLA_QK) + EPS)
        kn_ref[h] = jnp.concatenate([kh * rr * kg[:, :LANE], kpe_t * rr * kg[:, LANE:]], axis=-1).astype(BF16)

    qb = _mm(xn, win_ref[:, O_QB:O_QB + WIDTH_B])
    m_scale = MOBA_DIM ** -0.5
    for h in range(MOBA_HEADS):
        t = qb[:, h * MOBA_DIM:(h + 1) * MOBA_DIM]
        t = (t * _rms(t, MOBA_DIM)) * mqn_ref[...]
        qbf_ref[:, h * MOBA_DIM:(h + 1) * MOBA_DIM] = t
        qb16_ref[h] = (t * m_scale).astype(BF16)
    kbz = _mm(xn, win_ref[:, O_KB:O_KB + MOBA_KV_HEADS * MOBA_DIM])
    kparts = []
    for g in range(MOBA_KV_HEADS):
        t = kbz[:, g * MOBA_DIM:(g + 1) * MOBA_DIM]
        kparts.append((t * _rms(t, MOBA_DIM)) * mkn_ref[...])
    kb = jnp.concatenate(kparts, axis=-1)
    kb_ref[...] = kb
    kb16_ref[...] = kb.astype(BF16)
    kmean_ref[0] = jnp.mean(kb, axis=0, keepdims=True)
    vb = _mm(xn, win_ref[:, O_VB:O_VB + MOBA_KV_HEADS * MOBA_DIM])
    vb_ref[...] = vb
    vb16_ref[...] = vb.astype(BF16)

    sga_ref[...] = _silu(_mm(xn, win_ref[:, O_GA:O_GA + WIDTH_A])).astype(BF16)
    sgb_ref[...] = _silu(_mm(xn, win_ref[:, O_GB:O_GB + WIDTH_B])).astype(BF16)


def _even_project(x, cs, sn, w, q_dtype):
    rows = x.shape[0]
    tm = min(MOBA_BLOCK, rows)
    assert rows % tm == 0
    nt = rows // tm
    row = lambda width: pl.BlockSpec((tm, width), lambda i: (i, 0))
    head = lambda width: pl.BlockSpec((MLA_HEADS, tm, width), lambda i: (0, i, 0))
    kvw = MOBA_KV_HEADS * MOBA_DIM
    out_shape = (
        jax.ShapeDtypeStruct((MLA_HEADS, rows, MLA_QK_PAD), q_dtype),
        jax.ShapeDtypeStruct((MLA_HEADS, rows, MLA_QK_PAD), BF16),
        jax.ShapeDtypeStruct((rows, WIDTH_A), BF16),
        jax.ShapeDtypeStruct((rows, MLA_KV_LORA), F32),
        jax.ShapeDtypeStruct((rows, MLA_ROPE), F32),
        jax.ShapeDtypeStruct((MOBA_HEADS, rows, MOBA_DIM), BF16),
        jax.ShapeDtypeStruct((rows, WIDTH_B), F32),
        jax.ShapeDtypeStruct((rows, kvw), F32),
        jax.ShapeDtypeStruct((rows, kvw), F32),
        jax.ShapeDtypeStruct((rows, kvw), BF16),
        jax.ShapeDtypeStruct((rows, kvw), BF16),
        jax.ShapeDtypeStruct((nt, 1, kvw), F32),
        jax.ShapeDtypeStruct((rows, WIDTH_A), BF16),
        jax.ShapeDtypeStruct((rows, WIDTH_B), BF16),
    )
    out_specs = (
        head(MLA_QK_PAD), head(MLA_QK_PAD), row(WIDTH_A), row(MLA_KV_LORA), row(MLA_ROPE),
        head(MOBA_DIM), row(WIDTH_B), row(kvw), row(kvw), row(kvw), row(kvw),
        pl.BlockSpec((1, 1, kvw), lambda i: (i, 0, 0)), row(WIDTH_A), row(WIDTH_B),
    )
    in_specs = [
        row(D_MODEL), row(LANE), row(LANE), _const_spec((1, D_MODEL)), _const_spec((D_MODEL, EVEN_COLS)),
        _const_spec((1, MLA_Q_LORA)), _const_spec((MLA_Q_LORA, MLA_HEADS * UQ_HEAD_COLS)),
        _const_spec((1, MLA_KV_LORA)), _const_spec((1, MLA_QK_PAD)), _const_spec((1, MLA_QK_PAD)),
        _const_spec((1, MOBA_DIM)), _const_spec((1, MOBA_DIM)),
        _const_spec((MLA_KV_LORA, MLA_HEADS * (MLA_NOPE + MLA_V))),
    ]
    return pl.pallas_call(
        _proj_kernel, grid=(nt,), in_specs=in_specs, out_specs=out_specs, out_shape=out_shape,
        compiler_params=_params(("arbitrary",)),
    )(x, cs, sn, w["norm"], w["w_in"], w["qa_norm"], w["w_uq"], w["kva_norm"], w["q_gain"], w["k_gain"],
      w["moba_qn"], w["moba_kn"], w["w_ukv"])


def _gate_kernel(q_ref, km_ref, b_ref, *, nb):
    i = pl.program_id(0)
    nbp = -(-nb // SUBLANE) * SUBLANE
    tq = q_ref.shape[0]
    jrow = lax.broadcasted_iota(jnp.int32, (nbp, tq), 0)
    valid = jrow < i
    for h in range(MOBA_HEADS):
        g = h // MOBA_REP
        km = km_ref[:nbp, g * MOBA_DIM:(g + 1) * MOBA_DIM]
        qh = q_ref[:, h * MOBA_DIM:(h + 1) * MOBA_DIM]
        gate = _mm_nt(km, qh, precision=lax.Precision.HIGHEST)
        gate = jnp.where(valid, gate, -jnp.inf)
        rank = jnp.zeros((nbp, tq), F32)
        for o in range(nb):
            other = gate[o:o + 1, :]
            ahead = (other > gate) | ((other == gate) & (o < jrow))
            rank = rank + jnp.where(ahead, 1.0, 0.0)
        keep = ((rank < MOBA_TOPK) & valid) | (jrow == i)
        bias_t = jnp.where(keep, 0.0, NEG)
        bias_t = jnp.concatenate([bias_t, jnp.zeros((LANE - nbp, tq), F32)], axis=0)
        b_ref[h] = bias_t.T.astype(BF16)


def _moba_gate(qbf, kmean, nb):
    rows = qbf.shape[0]
    assert nb <= LANE and rows == nb * MOBA_BLOCK
    kvw = MOBA_KV_HEADS * MOBA_DIM
    km = jnp.zeros((LANE, kvw), F32).at[:nb].set(kmean.reshape(nb, kvw))
    return pl.pallas_call(
        functools.partial(_gate_kernel, nb=nb), grid=(nb,),
        in_specs=[pl.BlockSpec((MOBA_BLOCK, WIDTH_B), lambda i: (i, 0)), _const_spec((LANE, kvw))],
        out_specs=pl.BlockSpec((MOBA_HEADS, MOBA_BLOCK, LANE), lambda i: (0, i, 0)),
        out_shape=jax.ShapeDtypeStruct((MOBA_HEADS, rows, LANE), BF16),
        compiler_params=_params(("arbitrary",)),
    )(qbf, km)


def _tri_pairs(n):
    ii, jj = [], []
    for i in range(n):
        for j in range(i + 1):
            ii.append(i)
            jj.append(j)
    return jnp.asarray(ii, jnp.int32), jnp.asarray(jj, jnp.int32)


def _rep(x, width):
    return x if width == LANE else jnp.concatenate([x] * (width // LANE), axis=-1)


def _flash_update(scores, values, scratch):
    probs, alphas = [], []
    for s, (m_scr, l_scr, _) in zip(scores, scratch):
        m_prev = m_scr[...]
        m_new = jnp.maximum(m_prev, jnp.max(s, axis=-1, keepdims=True))
        alpha = jnp.exp(m_prev - m_new)
        p = jnp.exp(s - _rep(m_new, s.shape[-1]))
        l_scr[...] = alpha * l_scr[...] + jnp.sum(p, axis=-1, keepdims=True)
        m_scr[...] = m_new
        probs.append(p.astype(BF16))
        alphas.append(alpha)
    for p, alpha, v, (_, _, acc_scr) in zip(probs, alphas, values, scratch):
        acc_scr[...] = _rep(alpha, acc_scr.shape[-1]) * acc_scr[...] + _mm(p, v)


def _flash_init(scratch):
    for m_scr, l_scr, acc_scr in scratch:
        m_scr[...] = jnp.full(m_scr.shape, NEG, F32)
        l_scr[...] = jnp.zeros(l_scr.shape, F32)
        acc_scr[...] = jnp.zeros(acc_scr.shape, F32)


MLA_HEADS_PER_STEP = 2


def _mla_prompt_kernel(ii_ref, jj_ref, q_ref, k_ref, v_ref, o_ref, m_scr, l_scr, acc_scr):
    t = pl.program_id(1)
    i = ii_ref[t]
    j = jj_ref[t]
    tq, tk = q_ref.shape[1], k_ref.shape[1]
    scratch = [(m_scr.at[a], l_scr.at[a], acc_scr.at[a]) for a in range(MLA_HEADS_PER_STEP)]

    @pl.when(j == 0)
    def _():
        _flash_init(scratch)

    ahead = lax.broadcasted_iota(jnp.int32, (tq, tk), 1) - lax.broadcasted_iota(jnp.int32, (tq, tk), 0)
    visible = ahead <= (i * tq - j * tk)
    scores = [jnp.where(visible, _mm_nt(q_ref[a], k_ref[a]), NEG) for a in range(MLA_HEADS_PER_STEP)]
    values = [v_ref[:, a * MLA_V:(a + 1) * MLA_V] for a in range(MLA_HEADS_PER_STEP)]
    _flash_update(scores, values, scratch)

    @pl.when(j == i)
    def _():
        for a in range(MLA_HEADS_PER_STEP):
            o_ref[:, a * MLA_V:(a + 1) * MLA_V] = acc_scr[a] / l_scr[a]


def _mla_prompt(qa, kn, va):
    rows = qa.shape[1]
    t = min(512, rows)
    assert rows % t == 0
    hp = MLA_HEADS_PER_STEP
    ii, jj = _tri_pairs(rows // t)
    grid_spec = pltpu.PrefetchScalarGridSpec(
        num_scalar_prefetch=2, grid=(MLA_HEADS // hp, ii.shape[0]),
        in_specs=[
            pl.BlockSpec((hp, t, MLA_QK_PAD), lambda h, s, ii, jj: (h, ii[s], 0)),
            pl.BlockSpec((hp, t, MLA_QK_PAD), lambda h, s, ii, jj: (h, jj[s], 0)),
            pl.BlockSpec((t, hp * MLA_V), lambda h, s, ii, jj: (jj[s], h)),
        ],
        out_specs=pl.BlockSpec((t, hp * MLA_V), lambda h, s, ii, jj: (ii[s], h)),
        scratch_shapes=[pltpu.VMEM((hp, t, LANE), F32), pltpu.VMEM((hp, t, LANE), F32),
                        pltpu.VMEM((hp, t, MLA_V), F32)],
    )
    return pl.pallas_call(
        _mla_prompt_kernel, grid_spec=grid_spec, out_shape=jax.ShapeDtypeStruct((rows, WIDTH_A), F32),
        compiler_params=_params(("arbitrary", "arbitrary")),
    )(ii, jj, qa, kn, va)


def _moba_prompt_kernel(ii_ref, jj_ref, q_ref, b_ref, k_ref, v_ref, o_ref, m_scr, l_scr, acc_scr):
    t = pl.program_id(0)
    i = ii_ref[t]
    j = jj_ref[t]
    blk = MOBA_BLOCK
    nrow = MOBA_REP * blk
    scratch = [(m_scr.at[g], l_scr.at[g], acc_scr.at[g]) for g in range(MOBA_KV_HEADS)]

    @pl.when(j == 0)
    def _():
        _flash_init(scratch)

    onehot = jnp.where(lax.broadcasted_iota(jnp.int32, (blk, LANE), 1) == j, 1.0, 0.0).astype(BF16)
    qpos = lax.broadcasted_iota(jnp.int32, (MOBA_REP, blk, blk), 1).reshape(nrow, blk)
    kpos = lax.broadcasted_iota(jnp.int32, (nrow, blk), 1)
    visible = (j < i) | (kpos <= qpos)
    scores, values = [], []
    for g in range(MOBA_KV_HEADS):
        heads = slice(g * MOBA_REP, (g + 1) * MOBA_REP)
        q_aug = jnp.concatenate([q_ref[heads].reshape(nrow, MOBA_DIM), b_ref[heads].reshape(nrow, LANE)], axis=-1)
        k_aug = jnp.concatenate([k_ref[:, g * MOBA_DIM:(g + 1) * MOBA_DIM], onehot], axis=-1)
        scores.append(jnp.where(visible, _mm_nt(q_aug, k_aug), NEG))
        values.append(v_ref[:, g * MOBA_DIM:(g + 1) * MOBA_DIM])
    _flash_update(scores, values, scratch)

    @pl.when(j == i)
    def _():
        for g in range(MOBA_KV_HEADS):
            o = acc_scr[g] / l_scr[g]
            for r in range(MOBA_REP):
                h = g * MOBA_REP + r
                o_ref[:, h * MOBA_DIM:(h + 1) * MOBA_DIM] = o[r * blk:(r + 1) * blk]


def _moba_prompt(qb16, bias, kb16, vb16):
    rows = qb16.shape[1]
    nb = rows // MOBA_BLOCK
    ii, jj = _tri_pairs(nb)
    nrow = MOBA_REP * MOBA_BLOCK
    kvw = MOBA_KV_HEADS * MOBA_DIM
    grid_spec = pltpu.PrefetchScalarGridSpec(
        num_scalar_prefetch=2, grid=(ii.shape[0],),
        in_specs=[
            pl.BlockSpec((MOBA_HEADS, MOBA_BLOCK, MOBA_DIM), lambda s, ii, jj: (0, ii[s], 0)),
            pl.BlockSpec((MOBA_HEADS, MOBA_BLOCK, LANE), lambda s, ii, jj: (0, ii[s], 0)),
            pl.BlockSpec((MOBA_BLOCK, kvw), lambda s, ii, jj: (jj[s], 0)),
            pl.BlockSpec((MOBA_BLOCK, kvw), lambda s, ii, jj: (jj[s], 0)),
        ],
        out_specs=pl.BlockSpec((MOBA_BLOCK, WIDTH_B), lambda s, ii, jj: (ii[s], 0)),
        scratch_shapes=[pltpu.VMEM((MOBA_KV_HEADS, nrow, LANE), F32), pltpu.VMEM((MOBA_KV_HEADS, nrow, LANE), F32),
                        pltpu.VMEM((MOBA_KV_HEADS, nrow, MOBA_DIM), F32)],
    )
    return pl.pallas_call(
        _moba_prompt_kernel, grid_spec=grid_spec, out_shape=jax.ShapeDtypeStruct((rows, WIDTH_B), F32),
        compiler_params=_params(("arbitrary",)),
    )(ii, jj, qb16, bias, kb16, vb16)


PAGES_PER_STEP = 16
COMBINE_UNROLL = 8


def _paged_copies(pt_ref, srcs, dsts, sem, b, j, slot):
    copies = []
    for p in range(PAGES_PER_STEP):
        page = pt_ref[b, j * PAGES_PER_STEP + p]
        for a, (src, dst) in enumerate(zip(srcs, dsts)):
            copies.append((pltpu.make_async_copy(src.at[page], dst(slot, p), sem.at[a, slot]), a))
    return copies


def _paged_pipeline(pt_ref, srcs, dsts, sem):
    b = pl.program_id(0)
    j = pl.program_id(1)
    nj = pl.num_programs(1)
    step = b * nj + j
    slot = step % 2

    @pl.when(step == 0)
    def _():
        for cp, prio in _paged_copies(pt_ref, srcs, dsts, sem, b, j, slot):
            cp.start(priority=prio)

    @pl.when(step + 1 < pl.num_programs(0) * nj)
    def _():
        last = j + 1 == nj
        nb_ = jnp.where(last, b + 1, b)
        nj_ = jnp.where(last, 0, j + 1)
        for cp, prio in _paged_copies(pt_ref, srcs, dsts, sem, nb_, nj_, 1 - slot):
            cp.start(priority=prio)

    for cp, _ in _paged_copies(pt_ref, srcs, dsts, sem, b, j, slot):
        cp.wait()
    return slot


def _mla_paged_kernel(pt_ref, qbd_ref, qpe_ref, wukt_ref, wuv_ref, cnew_ref, knew_ref, lat_hbm, kpe_hbm,
                      o_ref, lat_buf, kpe_buf, sem, lhs_scr, m_scr, l_scr, ctx_scr, *, page_size):
    b = pl.program_id(0)
    j = pl.program_id(1)
    nj = pl.num_programs(1)
    nq = qbd_ref.shape[1]
    nhd = MLA_HEADS * MLA_NOPE
    scratch = [(m_scr, l_scr, ctx_scr)]

    @pl.when((b == 0) & (j == 0))
    def _():
        lhs_scr[:nhd, :] = wukt_ref[...]

    @pl.when(j == 0)
    def _():
        lhs_scr[nhd:, :] = _mm(qbd_ref[0], wukt_ref[...]).astype(BF16)
        _flash_init(scratch)

    slot = _paged_pipeline(
        pt_ref, (lat_hbm, kpe_hbm),
        (lambda s, p: lat_buf.at[s, pl.ds(p * page_size, page_size)],
         lambda s, p: kpe_buf.at[s, p]), sem)

    def project(c):
        cb = c.astype(BF16)
        return cb, _mm_nt(lhs_scr[...], cb)

    def attend(cb, res, kpe_t, mask):
        n = cb.shape[0]
        kn = res[:nhd]
        n2 = jnp.sum((kn * kn).reshape(MLA_HEADS, MLA_NOPE, n), axis=1)
        pe2 = jnp.sum(kpe_t * kpe_t, axis=0, keepdims=True)
        spe = _mm(qpe_ref[0], kpe_t.astype(BF16))
        r = lax.rsqrt((n2 + pe2) * (1.0 / MLA_QK) + EPS)
        s = ((res[nhd:] + spe).reshape(nq // MLA_HEADS, MLA_HEADS, n) * r[None]).reshape(nq, n)
        if mask is not None:
            s = jnp.where(mask, s, NEG)
        _flash_update([s], [cb], scratch)

    cb, res = project(lat_buf[slot])
    attend(cb, res, jnp.concatenate([kpe_buf[slot, p] for p in range(PAGES_PER_STEP)], axis=-1), None)

    @pl.when(j == nj - 1)
    def _():
        nnew = cnew_ref.shape[1]
        qidx = lax.broadcasted_iota(jnp.int32, (nq // MLA_HEADS, MLA_HEADS, nnew), 0).reshape(nq, nnew)
        kidx = lax.broadcasted_iota(jnp.int32, (nq, nnew), 1)
        cb, res = project(cnew_ref[0])
        attend(cb, res, knew_ref[0], kidx <= qidx)
        ctx = (ctx_scr[...] / _rep(l_scr[...], MLA_KV_LORA)).astype(BF16)
        full = _mm(ctx, wuv_ref[...])
        hrow = lax.broadcasted_iota(jnp.int32, (nq // MLA_HEADS, MLA_HEADS, MLA_V), 1).reshape(nq, MLA_V)
        out = jnp.zeros((nq, MLA_V), F32)
        for h in range(MLA_HEADS):
            out = out + jnp.where(hrow == h, full[:, h * MLA_V:(h + 1) * MLA_V], 0.0)
        o_ref[0] = out


def _mla_paged(page_table, qbd, qpe, wukt, wuv, cnew, knew_t, cache_lat, cache_kpe_t):
    ns, n_pages = page_table.shape
    page_size = cache_lat.shape[1]
    assert n_pages % PAGES_PER_STEP == 0
    nq = qbd.shape[1]
    keys = PAGES_PER_STEP * page_size
    nhd = MLA_HEADS * MLA_NOPE
    per_seq = lambda shape: pl.BlockSpec((1,) + shape, lambda b, j, pt: (b, 0, 0))
    grid_spec = pltpu.PrefetchScalarGridSpec(
        num_scalar_prefetch=1, grid=(ns, n_pages // PAGES_PER_STEP),
        in_specs=[
            per_seq((nq, nhd)), per_seq((nq, MLA_ROPE)),
            _const_spec((nhd, MLA_KV_LORA)), _const_spec((MLA_KV_LORA, MLA_HEADS * MLA_V)),
            per_seq(cnew.shape[1:]), per_seq(knew_t.shape[1:]),
            pl.BlockSpec(memory_space=pl.ANY), pl.BlockSpec(memory_space=pl.ANY),
        ],
        out_specs=per_seq((nq, MLA_V)),
        scratch_shapes=[
            pltpu.VMEM((2, keys, MLA_KV_LORA), F32), pltpu.VMEM((2, PAGES_PER_STEP, MLA_ROPE, page_size), F32),
            pltpu.SemaphoreType.DMA((2, 2)),
            pltpu.VMEM((nhd + nq, MLA_KV_LORA), BF16),
            pltpu.VMEM((nq, LANE), F32), pltpu.VMEM((nq, LANE), F32), pltpu.VMEM((nq, MLA_KV_LORA), F32),
        ],
    )
    return pl.pallas_call(
        functools.partial(_mla_paged_kernel, page_size=page_size), grid_spec=grid_spec,
        out_shape=jax.ShapeDtypeStruct((ns, nq, MLA_V), F32),
        compiler_params=_params(("arbitrary", "arbitrary")),
    )(page_table, qbd, qpe, wukt, wuv, cnew, knew_t, cache_lat, cache_kpe_t)


def _moba_paged_kernel(pt_ref, q16_ref, qf_ref, knew_ref, vnew_ref, k_hbm, v_hbm, o_ref,
                       k_buf, v_buf, sem, m_s, l_s, g_s, o_s, *, page_size, nb):
    j = pl.program_id(1)
    nj = pl.num_programs(1)
    nq = q16_ref.shape[1]
    blocks_per_step = PAGES_PER_STEP * page_size // MOBA_BLOCK
    page_rows = MOBA_KV_HEADS * page_size

    window = lambda buf: (lambda s, p: buf.at[s, pl.ds(p * page_rows, page_rows)])
    slot = _paged_pipeline(pt_ref, (k_hbm, v_hbm), (window(k_buf), window(v_buf)), sem)

    def block(buf, u):
        base = u * MOBA_KV_HEADS * MOBA_BLOCK
        return jnp.concatenate(
            [buf[slot, pl.ds(base + g, MOBA_BLOCK, stride=MOBA_KV_HEADS), :] for g in range(MOBA_KV_HEADS)], axis=-1)

    q16 = q16_ref[0]
    qf = qf_ref[0]
    wide = (nq, LANE)
    k16 = []
    for u in range(blocks_per_step):
        kf = block(k_buf, u)
        kmean = jnp.mean(kf, axis=0, keepdims=True)
        g_s[j * blocks_per_step + u] = jnp.broadcast_to(jnp.sum(qf * kmean, axis=-1, keepdims=True), wide)
        k16.append(kf.astype(BF16))
    s_all = _mm_nt(q16, jnp.concatenate(k16, axis=0))
    probs = []
    for u in range(blocks_per_step):
        jb = j * blocks_per_step + u
        s = s_all[:, u * MOBA_BLOCK:(u + 1) * MOBA_BLOCK]
        m = jnp.max(s, axis=-1, keepdims=True)
        p = jnp.exp(s - m)
        m_s[jb] = jnp.broadcast_to(m, wide)
        l_s[jb] = jnp.broadcast_to(jnp.sum(p, axis=-1, keepdims=True), wide)
        probs.append(p.astype(BF16))
    for u in range(blocks_per_step):
        o_s[j * blocks_per_step + u] = _mm(probs[u], block(v_buf, u).astype(BF16))

    @pl.when(j == nj - 1)
    def _():
        lane = lax.broadcasted_iota(jnp.int32, wide, 1)

        def gather(jb, carry):
            gm, mm, lm = carry
            hit = lane == jb
            return jnp.where(hit, g_s[jb], gm), jnp.where(hit, m_s[jb], mm), jnp.where(hit, l_s[jb], lm)
        gm, mm, lm = lax.fori_loop(
            0, nb, gather, (jnp.full(wide, -jnp.inf, F32), jnp.full(wide, NEG, F32), jnp.zeros(wide, F32)),
            unroll=COMBINE_UNROLL)

        def count(o, rank):
            other = g_s[o]
            ahead = (other > gm) | ((other == gm) & (o < lane))
            return rank + jnp.where(ahead, 1.0, 0.0)
        rank = lax.fori_loop(0, nb, count, jnp.zeros(wide, F32), unroll=COMBINE_UNROLL)
        keep = (rank < MOBA_TOPK) & (lane < nb)

        nnew = knew_ref.shape[1]
        qidx = lax.broadcasted_iota(jnp.int32, (nq // MOBA_HEADS, MOBA_HEADS, nnew), 0).reshape(nq, nnew)
        kidx = lax.broadcasted_iota(jnp.int32, (nq, nnew), 1)
        s_own = jnp.where(kidx <= qidx, _mm_nt(q16, knew_ref[0].astype(BF16)), NEG)
        m_own = jnp.max(s_own, axis=-1, keepdims=True)
        m_tot = jnp.maximum(jnp.max(jnp.where(keep, mm, NEG), axis=-1, keepdims=True), m_own)
        p_own = jnp.exp(s_own - m_tot)
        wgt = jnp.exp(jnp.where(keep, mm - m_tot, NEG))
        l_tot = jnp.sum(wgt * lm, axis=-1, keepdims=True) + jnp.sum(p_own, axis=-1, keepdims=True)
        acc0 = _mm(p_own.astype(BF16), vnew_ref[0].astype(BF16))

        def combine(jb, acc):
            w_col = jnp.sum(jnp.where(lane == jb, wgt, 0.0), axis=-1, keepdims=True)
            return acc + w_col * o_s[jb]
        acc = lax.fori_loop(0, nb, combine, acc0, unroll=COMBINE_UNROLL) / l_tot
        grow = lax.broadcasted_iota(jnp.int32, (nq // MOBA_HEADS, MOBA_HEADS, MOBA_DIM), 1).reshape(nq, MOBA_DIM)
        o_ref[0] = jnp.where(grow < MOBA_REP, acc[:, :MOBA_DIM], acc[:, MOBA_DIM:])


def _moba_paged(page_table, q16, qf, knew, vnew, cache_k, cache_v, page_size):
    ns, n_pages = page_table.shape
    past = n_pages * page_size
    assert n_pages % PAGES_PER_STEP == 0 and past % MOBA_BLOCK == 0
    assert (PAGES_PER_STEP * page_size) % MOBA_BLOCK == 0
    nb = past // MOBA_BLOCK
    assert nb <= LANE
    nq = q16.shape[1]
    kvw = MOBA_KV_HEADS * MOBA_DIM
    rows = PAGES_PER_STEP * page_size * MOBA_KV_HEADS
    per_seq = lambda shape: pl.BlockSpec((1,) + shape, lambda b, j, pt: (b, 0, 0))
    grid_spec = pltpu.PrefetchScalarGridSpec(
        num_scalar_prefetch=1, grid=(ns, n_pages // PAGES_PER_STEP),
        in_specs=[
            per_seq((nq, kvw)), per_seq((nq, kvw)), per_seq(knew.shape[1:]), per_seq(vnew.shape[1:]),
            pl.BlockSpec(memory_space=pl.ANY), pl.BlockSpec(memory_space=pl.ANY),
        ],
        out_specs=per_seq((nq, MOBA_DIM)),
        scratch_shapes=[
            pltpu.VMEM((2, rows, MOBA_DIM), F32), pltpu.VMEM((2, rows, MOBA_DIM), F32),
            pltpu.SemaphoreType.DMA((2, 2)),
            pltpu.VMEM((nb, nq, LANE), F32), pltpu.VMEM((nb, nq, LANE), F32), pltpu.VMEM((nb, nq, LANE), F32),
            pltpu.VMEM((nb, nq, kvw), F32),
        ],
    )
    return pl.pallas_call(
        functools.partial(_moba_paged_kernel, page_size=page_size, nb=nb), grid_spec=grid_spec,
        out_shape=jax.ShapeDtypeStruct((ns, nq, MOBA_DIM), F32),
        compiler_params=_params(("arbitrary", "arbitrary")),
    )(page_table, q16, qf, knew, vnew, cache_k, cache_v)


def _merge_kernel(x_ref, oa_ref, ob_ref, sga_ref, sgb_ref, w_ref, o_ref):
    mix = jnp.concatenate([(oa_ref[...] * sga_ref[...].astype(F32)).astype(BF16),
                           (ob_ref[...] * sgb_ref[...].astype(F32)).astype(BF16)], axis=-1)
    o_ref[...] = x_ref[...] + _mm(mix, w_ref[...])


def _even_merge(x, oa, ob, sga, sgb, w_out):
    rows = x.shape[0]
    tm = min(256, rows)
    assert rows % tm == 0
    row = lambda width: pl.BlockSpec((tm, width), lambda i: (i, 0))
    return pl.pallas_call(
        _merge_kernel, grid=(rows // tm,),
        in_specs=[row(D_MODEL), row(WIDTH_A), row(WIDTH_B), row(WIDTH_A), row(WIDTH_B),
                  _const_spec((WIDTH_A + WIDTH_B, D_MODEL))],
        out_specs=row(D_MODEL), out_shape=jax.ShapeDtypeStruct((rows, D_MODEL), F32),
        compiler_params=_params(("arbitrary",)),
    )(x, oa, ob, sga, sgb, w_out)


def _conv_kernel(x_ref, norm_ref, wb_ref, wc_ref, wh_ref, wg_ref, cw_ref, wo_ref, prev_ref,
                 o_ref, st_ref, xn_scr, acc_scr, carry_scr, *, step_major_seqs):
    i = pl.program_id(0)
    n = pl.program_id(1)
    tm = x_ref.shape[0]

    @pl.when(n == 0)
    def _():
        x = x_ref[...]
        xn_scr[...] = ((x * _rms(x, D_MODEL)) * norm_ref[...]).astype(BF16)
        acc_scr[...] = x

    xn = xn_scr[...]
    ch = _mm(xn, wc_ref[...]) * _mm(xn, wh_ref[...])
    cw = cw_ref[...]
    if step_major_seqs:
        ns = step_major_seqs
        full = jnp.concatenate([prev_ref[...], ch], axis=0)
        conv = cw[0:1] * full[0:tm] + cw[1:2] * full[ns:ns + tm] + cw[2:3] * full[2 * ns:2 * ns + tm]
        st_ref[...] = ch[tm - 2 * ns:]
    else:
        @pl.when(i == 0)
        def _():
            carry_scr[n] = jnp.zeros(carry_scr.shape[1:], F32)

        prev = carry_scr[n]
        rowi = lax.broadcasted_iota(jnp.int32, ch.shape, 0)
        m1 = jnp.where(rowi == 0, prev[SUBLANE - 1:SUBLANE], pltpu.roll(ch, 1, 0))
        m2 = jnp.where(rowi == 0, prev[SUBLANE - 2:SUBLANE - 1],
                       jnp.where(rowi == 1, prev[SUBLANE - 1:SUBLANE], pltpu.roll(ch, 2, 0)))
        conv = cw[0:1] * m2 + cw[1:2] * m1 + cw[2:3] * ch
        carry_scr[n] = ch[tm - SUBLANE:]
        st_ref[...] = ch[tm - SUBLANE:]
    y = (_mm(xn, wb_ref[...]) * conv * _silu(_mm(xn, wg_ref[...]))).astype(BF16)
    acc_scr[...] += _mm(y, wo_ref[...])

    @pl.when(n == pl.num_programs(1) - 1)
    def _():
        o_ref[...] = acc_scr[...]


def _conv_layer(x, prev, w, step_major_seqs):
    rows = x.shape[0]
    tm = rows if step_major_seqs else min(512, rows)
    tn = 512
    assert rows % tm == 0
    ncol = CONV_DIM // tn
    st_rows = prev.shape[0] if step_major_seqs else SUBLANE
    wcol = lambda k: pl.BlockSpec((D_MODEL, tn), lambda i, n, k=k: (0, n + k * ncol))
    return pl.pallas_call(
        functools.partial(_conv_kernel, step_major_seqs=step_major_seqs), grid=(rows // tm, ncol),
        in_specs=[
            pl.BlockSpec((tm, D_MODEL), lambda i, n: (i, 0)), _const_spec((1, D_MODEL)),
            wcol(0), wcol(1), wcol(2), wcol(3),
            pl.BlockSpec((CONV_WIDTH, tn), lambda i, n: (0, n)),
            pl.BlockSpec((tn, D_MODEL), lambda i, n: (n, 0)),
            pl.BlockSpec((prev.shape[0], tn), lambda i, n: (0, n)),
        ],
        out_specs=(pl.BlockSpec((tm, D_MODEL), lambda i, n: (i, 0)),
                   pl.BlockSpec((st_rows, tn), lambda i, n: (i, n))),
        out_shape=(jax.ShapeDtypeStruct((rows, D_MODEL), F32),
                   jax.ShapeDtypeStruct((rows // tm * st_rows, CONV_DIM), F32)),
        scratch_shapes=[pltpu.VMEM((tm, D_MODEL), BF16), pltpu.VMEM((tm, D_MODEL), F32),
                        pltpu.VMEM((ncol, SUBLANE, tn), F32)],
        compiler_params=_params(("arbitrary", "arbitrary")),
    )(x, w["norm"], w["w_in"], w["w_in"], w["w_in"], w["w_in"], w["conv_w"], w["w_out"], prev)


def _rot_cols(w):
    half = MLA_ROPE // 2
    return jnp.concatenate([-w[..., half:], w[..., :half]], axis=-1)


def _pad_cols(w, width):
    return jnp.pad(w, ((0, 0), (0, width - w.shape[-1])))


def _gain_pad(g):
    full = jnp.concatenate([g[:MLA_NOPE], g[MLA_NOPE:], g[MLA_NOPE:]])
    return jnp.pad(full, (0, MLA_QK_PAD - MLA_QK)).reshape(1, MLA_QK_PAD)


def _even_weights(norm, w_in, qa_norm, w_uq, kva_norm, w_uk, w_uv, q_norm, k_norm, moba_qn, moba_kn):
    o = np.cumsum((0, MLA_Q_LORA, MLA_KV_LORA, MLA_ROPE, WIDTH_B, MOBA_KV_HEADS * MOBA_DIM,
                   MOBA_KV_HEADS * MOBA_DIM, WIDTH_A, WIDTH_B))
    kpe_w = w_in[:, o[2]:o[3]]
    w_in2 = jnp.concatenate([
        w_in[:, o[0]:o[2]], _pad_cols(kpe_w, LANE), _pad_cols(_rot_cols(kpe_w), LANE), w_in[:, o[3]:],
    ], axis=-1).astype(BF16)
    uq = w_uq.reshape(MLA_Q_LORA, MLA_HEADS, MLA_QK)
    pe_w = uq[..., MLA_NOPE:]
    pad = jnp.zeros((MLA_Q_LORA, MLA_HEADS, LANE - MLA_ROPE), F32)
    uq2 = jnp.concatenate([uq[..., :MLA_NOPE], pe_w, pad, _rot_cols(pe_w), pad], axis=-1)
    return {
        "norm": norm.reshape(1, D_MODEL), "w_in": w_in2, "qa_norm": qa_norm.reshape(1, MLA_Q_LORA),
        "w_uq": uq2.reshape(MLA_Q_LORA, MLA_HEADS * UQ_HEAD_COLS).astype(BF16),
        "kva_norm": kva_norm.reshape(1, MLA_KV_LORA), "q_gain": _gain_pad(q_norm), "k_gain": _gain_pad(k_norm),
        "moba_qn": moba_qn.reshape(1, MOBA_DIM), "moba_kn": moba_kn.reshape(1, MOBA_DIM),
        "w_ukv": jnp.concatenate([w_uk, w_uv], axis=-1).astype(BF16),
    }


def _rope_tables(pos):
    half = MLA_ROPE // 2
    inv = ROPE_THETA ** (-jnp.arange(half, dtype=F32) / half)
    ang = pos.astype(F32)[:, None] * inv
    cos, sin = jnp.cos(ang), jnp.sin(ang)
    pad = jnp.zeros((pos.shape[0], LANE - MLA_ROPE), F32)
    return jnp.concatenate([cos, cos, pad], axis=-1), jnp.concatenate([sin, sin, pad], axis=-1)


def _block_diag_rows(x, nblk):
    eye = jnp.eye(nblk, dtype=x.dtype)
    out = x[..., :, None, :] * eye[:, :, None]
    return out.reshape(x.shape[:-1] + (nblk * x.shape[-1],))


def kernel(x_prompt, x_sample, cache_mla_latent, cache_mla_kpe, cache_moba_k, cache_moba_v, state_conv, page_table, norm_even, w_in_even, mla_qa_norm, mla_w_uq, mla_kva_norm, mla_w_uk, mla_w_uv, mla_q_norm, mla_k_norm, moba_q_norm, moba_k_norm, w_out_even, norm_odd, w_in_odd, conv_w, w_out_odd):
    assert x_prompt.shape[0] == 1 and norm_even.shape[0] == 1 and norm_odd.shape[0] == 1
    seq = x_prompt.shape[1]
    ns, steps = x_sample.shape[:2]
    n_pages = page_table.shape[1]
    page_size = cache_mla_latent.shape[2]
    past = n_pages * page_size
    assert seq % MOBA_BLOCK == 0 and steps <= SUBLANE
    kvw = MOBA_KV_HEADS * MOBA_DIM

    we = _even_weights(norm_even[0], w_in_even[0], mla_qa_norm[0], mla_w_uq[0], mla_kva_norm[0], mla_w_uk[0],
                       mla_w_uv[0], mla_q_norm[0], mla_k_norm[0], moba_q_norm[0], moba_k_norm[0])
    w_out_e = w_out_even[0].astype(BF16)
    wo = {"norm": norm_odd[0].reshape(1, D_MODEL), "w_in": w_in_odd[0].astype(BF16), "conv_w": conv_w[0],
          "w_out": w_out_odd[0].astype(BF16)}

    xp = x_prompt[0]
    cs, sn = _rope_tables(jnp.arange(seq))
    (qa, kn, va, lat_p, kpe_p, qb16, qbf, kb_p, vb_p, kb16, vb16, kmean, sga, sgb) = _even_project(xp, cs, sn, we, BF16)
    oa = _mla_prompt(qa, kn, va)
    bias = _moba_gate(qbf, kmean, seq // MOBA_BLOCK)
    ob = _moba_prompt(qb16, bias, kb16, vb16)
    y1p = _even_merge(xp, oa, ob, sga, sgb, w_out_e)
    y2p, st_p = _conv_layer(y1p, jnp.zeros((SUBLANE, CONV_DIM), F32), wo, 0)

    rows_s = steps * ns
    xs = x_sample.transpose(1, 0, 2).reshape(rows_s, D_MODEL)
    cs_s, sn_s = _rope_tables(past + jnp.repeat(jnp.arange(steps), ns))
    (qa_s, _, _, lat_s, kpe_s, qb16_s, qbf_s, kb_s, vb_s, _, _, _, sga_s, sgb_s) = _even_project(xs, cs_s, sn_s, we, F32)

    def per_seq(a):
        return a.reshape(steps, ns, a.shape[-1]).transpose(1, 0, 2)

    def pad_new(a):
        return jnp.pad(a, ((0, 0), (0, LANE - steps), (0, 0)))

    nq = steps * MLA_HEADS
    qk = qa_s.reshape(MLA_HEADS, steps, ns, MLA_QK_PAD).transpose(2, 1, 0, 3) * we["k_gain"][0]
    qbd = _block_diag_rows(qk[..., :MLA_NOPE], MLA_HEADS).reshape(ns, nq, MLA_HEADS * MLA_NOPE).astype(BF16)
    qpe = qk[..., MLA_NOPE:MLA_QK].reshape(ns, nq, MLA_ROPE).astype(BF16)
    oa_s = _mla_paged(page_table, qbd, qpe, mla_w_uk[0].T.astype(BF16), mla_w_uv[0].astype(BF16),
                      pad_new(per_seq(lat_s)), pad_new(per_seq(kpe_s)).transpose(0, 2, 1),
                      cache_mla_latent[0], cache_mla_kpe[0].transpose(0, 2, 1))

    def group_diag(a):
        g = a.reshape(ns, steps, MOBA_KV_HEADS, MOBA_REP, MOBA_DIM).transpose(0, 1, 3, 2, 4)
        return _block_diag_rows(g, MOBA_KV_HEADS).transpose(0, 1, 3, 2, 4).reshape(ns, nq, kvw)

    q16_s = group_diag(qb16_s.reshape(MOBA_HEADS, steps, ns, MOBA_DIM).transpose(2, 1, 0, 3))
    qf_s = group_diag(per_seq(qbf_s).reshape(ns, steps, MOBA_HEADS, MOBA_DIM))
    ob_s = _moba_paged(page_table, q16_s, qf_s, pad_new(per_seq(kb_s)), pad_new(per_seq(vb_s)),
                       cache_moba_k[0].reshape(-1, page_size * MOBA_KV_HEADS, MOBA_DIM),
                       cache_moba_v[0].reshape(-1, page_size * MOBA_KV_HEADS, MOBA_DIM), page_size)

    def step_major(o):
        return o.reshape(ns, steps, -1).transpose(1, 0, 2).reshape(rows_s, -1)

    y1s = _even_merge(xs, step_major(oa_s), step_major(ob_s), sga_s, sgb_s, w_out_e)
    prev_s = state_conv[0].transpose(1, 0, 2).reshape((CONV_WIDTH - 1) * ns, CONV_DIM)
    y2s, st_s = _conv_layer(y1s, prev_s, wo, ns)

    return (
        y2p[None], per_seq(y2s),
        lat_p[None, None], kpe_p[None, None],
        kb_p.reshape(1, 1, seq, MOBA_KV_HEADS, MOBA_DIM), vb_p.reshape(1, 1, seq, MOBA_KV_HEADS, MOBA_DIM),
        st_p[st_p.shape[0] - (CONV_WIDTH - 1):][None, None],
        per_seq(lat_s)[None], per_seq(kpe_s)[None],
        per_seq(kb_s).reshape(1, ns, steps, MOBA_KV_HEADS, MOBA_DIM),
        per_seq(vb_s).reshape(1, ns, steps, MOBA_KV_HEADS, MOBA_DIM),
        st_s.reshape(CONV_WIDTH - 1, ns, CONV_DIM).transpose(1, 0, 2)[None],
    )
```

```python
import functools

import jax
import jax.numpy as jnp
import numpy as np
from jax import lax
from jax.experimental import pallas as pl
from jax.experimental.pallas import tpu as pltpu

F32 = jnp.float32
BF16 = jnp.bfloat16

D_MODEL = 2048
MLA_HEADS = 8
MLA_NOPE = 128
MLA_ROPE = 64
MLA_QK = MLA_NOPE + MLA_ROPE
MLA_QK_PAD = 256
MLA_V = 128
MLA_Q_LORA = 512
MLA_KV_LORA = 512
ROPE_THETA = 10000.0
MOBA_HEADS = 8
MOBA_KV_HEADS = 2
MOBA_REP = MOBA_HEADS // MOBA_KV_HEADS
MOBA_DIM = 128
MOBA_BLOCK = 256
MOBA_TOPK = 3
CONV_DIM = D_MODEL
CONV_WIDTH = 3
WIDTH_A = MLA_HEADS * MLA_V
WIDTH_B = MOBA_HEADS * MOBA_DIM
EPS = 1e-6
NEG = -1e30
LANE = 128
SUBLANE = 8
VMEM_LIMIT = 56 * 1024 * 1024

O_QLAT = 0
O_C = O_QLAT + MLA_Q_LORA
O_KPE = O_C + MLA_KV_LORA
O_QB = O_KPE + 2 * LANE
O_KB = O_QB + WIDTH_B
O_VB = O_KB + MOBA_KV_HEADS * MOBA_DIM
O_GA = O_VB + MOBA_KV_HEADS * MOBA_DIM
O_GB = O_GA + WIDTH_A
EVEN_COLS = O_GB + WIDTH_B
UQ_HEAD_COLS = 3 * LANE


def _mm(a, b):
    return jnp.dot(a, b, preferred_element_type=F32)


def _mm_nt(a, b, precision=None):
    return lax.dot_general(a, b, (((1,), (1,)), ((), ())), preferred_element_type=F32, precision=precision)


def _const_spec(shape):
    return pl.BlockSpec(shape, lambda *_: (0,) * len(shape), pipeline_mode=pl.Buffered(1))


def _params(sem):
    return pltpu.CompilerParams(dimension_semantics=sem, vmem_limit_bytes=VMEM_LIMIT)


def _rms(x, width):
    return lax.rsqrt(jnp.sum(x * x, axis=-1, keepdims=True) * (1.0 / width) + EPS)


def _silu(x):
    return x * jax.nn.sigmoid(x)


def _proj_kernel(x_ref, cs_ref, sn_ref, norm_ref, win_ref, qan_ref, wuq_ref, kvan_ref, qg_ref, kg_ref,
                 mqn_ref, mkn_ref, wukv_ref,
                 qa_ref, kn_ref, va_ref, lat_ref, kpe_ref, qb16_ref, qbf_ref, kb_ref, vb_ref, kb16_ref,
                 vb16_ref, kmean_ref, sga_ref, sgb_ref):
    x = x_ref[...]
    xn = ((x * _rms(x, D_MODEL)) * norm_ref[...]).astype(BF16)
    cs = cs_ref[...]
    sn = sn_ref[...]

    ql = _mm(xn, win_ref[:, O_QLAT:O_QLAT + MLA_Q_LORA])
    qln = ((ql * _rms(ql, MLA_Q_LORA)) * qan_ref[...]).astype(BF16)
    qh_all = _mm(qln, wuq_ref[...])
    qg = qg_ref[...]
    q_scale = MLA_QK ** -0.5
    for h in range(MLA_HEADS):
        base = h * UQ_HEAD_COLS
        nope = qh_all[:, base:base + LANE]
        pe = qh_all[:, base + LANE:base + 2 * LANE] * cs + qh_all[:, base + 2 * LANE:base + 3 * LANE] * sn
        ssq = jnp.sum(nope * nope, axis=-1, keepdims=True) + jnp.sum(pe * pe, axis=-1, keepdims=True)
        rr = lax.rsqrt(ssq * (1.0 / MLA_QK) + EPS) * q_scale
        qa_ref[h] = jnp.concatenate([nope * rr * qg[:, :LANE], pe * rr * qg[:, LANE:]], axis=-1).astype(qa_ref.dtype)

    cz = _mm(xn, win_ref[:, O_C:O_C + MLA_KV_LORA])
    c = (cz * _rms(cz, MLA_KV_LORA)) * kvan_ref[...]
    lat_ref[...] = c
    kp = _mm(xn, win_ref[:, O_KPE:O_KPE + 2 * LANE])
    kpe_t = kp[:, :LANE] * cs + kp[:, LANE:] * sn
    kpe_ref[...] = kpe_t[:, :MLA_ROPE]
    kv = _mm(c.astype(BF16), wukv_ref[...])
    va_ref[...] = kv[:, MLA_HEADS * MLA_NOPE:].astype(BF16)
    pes = jnp.sum(kpe_t * kpe_t, axis=-1, keepdims=True)
    kg = kg_ref[...]
    for h in range(MLA_HEADS):
        kh = kv[:, h * MLA_NOPE:(h + 1) * MLA_NOPE]
        rr = lax.rsqrt((jnp.sum(kh * kh, axis=-1, keepdims=True) + pes) * (1.0 / MLA_QK) + EPS)
        kn_ref[h] = jnp.concatenate([kh * rr * kg[:, :LANE], kpe_t * rr * kg[:, LANE:]], axis=-1).astype(BF16)

    qb = _mm(xn, win_ref[:, O_QB:O_QB + WIDTH_B])
    m_scale = MOBA_DIM ** -0.5
    for h in range(MOBA_HEADS):
        t = qb[:, h * MOBA_DIM:(h + 1) * MOBA_DIM]
        t = (t * _rms(t, MOBA_DIM)) * mqn_ref[...]
        qbf_ref[:, h * MOBA_DIM:(h + 1) * MOBA_DIM] = t
        qb16_ref[h] = (t * m_scale).astype(BF16)
    kbz = _mm(xn, win_ref[:, O_KB:O_KB + MOBA_KV_HEADS * MOBA_DIM])
    kparts = []
    for g in range(MOBA_KV_HEADS):
        t = kbz[:, g * MOBA_DIM:(g + 1) * MOBA_DIM]
        kparts.append((t * _rms(t, MOBA_DIM)) * mkn_ref[...])
    kb = jnp.concatenate(kparts, axis=-1)
    kb_ref[...] = kb
    kb16_ref[...] = kb.astype(BF16)
    kmean_ref[0] = jnp.mean(kb, axis=0, keepdims=True)
    vb = _mm(xn, win_ref[:, O_VB:O_VB + MOBA_KV_HEADS * MOBA_DIM])
    vb_ref[...] = vb
    vb16_ref[...] = vb.astype(BF16)

    sga_ref[...] = _silu(_mm(xn, win_ref[:, O_GA:O_GA + WIDTH_A])).astype(BF16)
    sgb_ref[...] = _silu(_mm(xn, win_ref[:, O_GB:O_GB + WIDTH_B])).astype(BF16)


def _even_project(x, cs, sn, w, q_dtype):
    rows = x.shape[0]
    tm = min(MOBA_BLOCK, rows)
    assert rows % tm == 0
    nt = rows // tm
    row = lambda width: pl.BlockSpec((tm, width), lambda i: (i, 0))
    head = lambda width: pl.BlockSpec((MLA_HEADS, tm, width), lambda i: (0, i, 0))
    kvw = MOBA_KV_HEADS * MOBA_DIM
    out_shape = (
        jax.ShapeDtypeStruct((MLA_HEADS, rows, MLA_QK_PAD), q_dtype),
        jax.ShapeDtypeStruct((MLA_HEADS, rows, MLA_QK_PAD), BF16),
        jax.ShapeDtypeStruct((rows, WIDTH_A), BF16),
        jax.ShapeDtypeStruct((rows, MLA_KV_LORA), F32),
        jax.ShapeDtypeStruct((rows, MLA_ROPE), F32),
        jax.ShapeDtypeStruct((MOBA_HEADS, rows, MOBA_DIM), BF16),
        jax.ShapeDtypeStruct((rows, WIDTH_B), F32),
        jax.ShapeDtypeStruct((rows, kvw), F32),
        jax.ShapeDtypeStruct((rows, kvw), F32),
        jax.ShapeDtypeStruct((rows, kvw), BF16),
        jax.ShapeDtypeStruct((rows, kvw), BF16),
        jax.ShapeDtypeStruct((nt, 1, kvw), F32),
        jax.ShapeDtypeStruct((rows, WIDTH_A), BF16),
        jax.ShapeDtypeStruct((rows, WIDTH_B), BF16),
    )
    out_specs = (
        head(MLA_QK_PAD), head(MLA_QK_PAD), row(WIDTH_A), row(MLA_KV_LORA), row(MLA_ROPE),
        head(MOBA_DIM), row(WIDTH_B), row(kvw), row(kvw), row(kvw), row(kvw),
        pl.BlockSpec((1, 1, kvw), lambda i: (i, 0, 0)), row(WIDTH_A), row(WIDTH_B),
    )
    in_specs = [
        row(D_MODEL), row(LANE), row(LANE), _const_spec((1, D_MODEL)), _const_spec((D_MODEL, EVEN_COLS)),
        _const_spec((1, MLA_Q_LORA)), _const_spec((MLA_Q_LORA, MLA_HEADS * UQ_HEAD_COLS)),
        _const_spec((1, MLA_KV_LORA)), _const_spec((1, MLA_QK_PAD)), _const_spec((1, MLA_QK_PAD)),
        _const_spec((1, MOBA_DIM)), _const_spec((1, MOBA_DIM)),
        _const_spec((MLA_KV_LORA, MLA_HEADS * (MLA_NOPE + MLA_V))),
    ]
    return pl.pallas_call(
        _proj_kernel, grid=(nt,), in_specs=in_specs, out_specs=out_specs, out_shape=out_shape,
        compiler_params=_params(("arbitrary",)),
    )(x, cs, sn, w["norm"], w["w_in"], w["qa_norm"], w["w_uq"], w["kva_norm"], w["q_gain"], w["k_gain"],
      w["moba_qn"], w["moba_kn"], w["w_ukv"])


def _gate_kernel(q_ref, km_ref, b_ref, *, nb):
    i = pl.program_id(0)
    nbp = -(-nb // SUBLANE) * SUBLANE
    tq = q_ref.shape[0]
    jrow = lax.broadcasted_iota(jnp.int32, (nbp, tq), 0)
    valid = jrow < i
    for h in range(MOBA_HEADS):
        g = h // MOBA_REP
        km = km_ref[:nbp, g * MOBA_DIM:(g + 1) * MOBA_DIM]
        qh = q_ref[:, h * MOBA_DIM:(h + 1) * MOBA_DIM]
        gate = _mm_nt(km, qh, precision=lax.Precision.HIGHEST)
        gate = jnp.where(valid, gate, -jnp.inf)
        rank = jnp.zeros((nbp, tq), F32)
        for o in range(nb):
            other = gate[o:o + 1, :]
            ahead = (other > gate) | ((other == gate) & (o < jrow))
            rank = rank + jnp.where(ahead, 1.0, 0.0)
        keep = ((rank < MOBA_TOPK) & valid) | (jrow == i)
        bias_t = jnp.where(keep, 0.0, NEG)
        bias_t = jnp.concatenate([bias_t, jnp.zeros((LANE - nbp, tq), F32)], axis=0)
        b_ref[h] = bias_t.T.astype(BF16)


def _moba_gate(qbf, kmean, nb):
    rows = qbf.shape[0]
    assert nb <= LANE and rows == nb * MOBA_BLOCK
    kvw = MOBA_KV_HEADS * MOBA_DIM
    km = jnp.zeros((LANE, kvw), F32).at[:nb].set(kmean.reshape(nb, kvw))
    return pl.pallas_call(
        functools.partial(_gate_kernel, nb=nb), grid=(nb,),
        in_specs=[pl.BlockSpec((MOBA_BLOCK, WIDTH_B), lambda i: (i, 0)), _const_spec((LANE, kvw))],
        out_specs=pl.BlockSpec((MOBA_HEADS, MOBA_BLOCK, LANE), lambda i: (0, i, 0)),
        out_shape=jax.ShapeDtypeStruct((MOBA_HEADS, rows, LANE), BF16),
        compiler_params=_params(("arbitrary",)),
    )(qbf, km)


def _tri_pairs(n):
    ii, jj = [], []
    for i in range(n):
        for j in range(i + 1):
            ii.append(i)
            jj.append(j)
    return jnp.asarray(ii, jnp.int32), jnp.asarray(jj, jnp.int32)


def _rep(x, width):
    return x if width == LANE else jnp.concatenate([x] * (width // LANE), axis=-1)


def _flash_update(scores, values, scratch):
    probs, alphas = [], []
    for s, (m_scr, l_scr, _) in zip(scores, scratch):
        m_prev = m_scr[...]
        m_new = jnp.maximum(m_prev, jnp.max(s, axis=-1, keepdims=True))
        alpha = jnp.exp(m_prev - m_new)
        p = jnp.exp(s - _rep(m_new, s.shape[-1]))
        l_scr[...] = alpha * l_scr[...] + jnp.sum(p, axis=-1, keepdims=True)
        m_scr[...] = m_new
        probs.append(p.astype(BF16))
        alphas.append(alpha)
    for p, alpha, v, (_, _, acc_scr) in zip(probs, alphas, values, scratch):
        acc_scr[...] = _rep(alpha, acc_scr.shape[-1]) * acc_scr[...] + _mm(p, v)


def _flash_init(scratch):
    for m_scr, l_scr, acc_scr in scratch:
        m_scr[...] = jnp.full(m_scr.shape, NEG, F32)
        l_scr[...] = jnp.zeros(l_scr.shape, F32)
        acc_scr[...] = jnp.zeros(acc_scr.shape, F32)


MLA_HEADS_PER_STEP = 2


def _mla_prompt_kernel(ii_ref, jj_ref, q_ref, k_ref, v_ref, o_ref, m_scr, l_scr, acc_scr):
    t = pl.program_id(1)
    i = ii_ref[t]
    j = jj_ref[t]
    tq, tk = q_ref.shape[1], k_ref.shape[1]
    scratch = [(m_scr.at[a], l_scr.at[a], acc_scr.at[a]) for a in range(MLA_HEADS_PER_STEP)]

    @pl.when(j == 0)
    def _():
        _flash_init(scratch)

    ahead = lax.broadcasted_iota(jnp.int32, (tq, tk), 1) - lax.broadcasted_iota(jnp.int32, (tq, tk), 0)
    visible = ahead <= (i * tq - j * tk)
    scores = [jnp.where(visible, _mm_nt(q_ref[a], k_ref[a]), NEG) for a in range(MLA_HEADS_PER_STEP)]
    values = [v_ref[:, a * MLA_V:(a + 1) * MLA_V] for a in range(MLA_HEADS_PER_STEP)]
    _flash_update(scores, values, scratch)

    @pl.when(j == i)
    def _():
        for a in range(MLA_HEADS_PER_STEP):
            o_ref[:, a * MLA_V:(a + 1) * MLA_V] = acc_scr[a] / l_scr[a]


def _mla_prompt(qa, kn, va):
    rows = qa.shape[1]
    t = min(512, rows)
    assert rows % t == 0
    hp = MLA_HEADS_PER_STEP
    ii, jj = _tri_pairs(rows // t)
    grid_spec = pltpu.PrefetchScalarGridSpec(
        num_scalar_prefetch=2, grid=(MLA_HEADS // hp, ii.shape[0]),
        in_specs=[
            pl.BlockSpec((hp, t, MLA_QK_PAD), lambda h, s, ii, jj: (h, ii[s], 0)),
            pl.BlockSpec((hp, t, MLA_QK_PAD), lambda h, s, ii, jj: (h, jj[s], 0)),
            pl.BlockSpec((t, hp * MLA_V), lambda h, s, ii, jj: (jj[s], h)),
        ],
        out_specs=pl.BlockSpec((t, hp * MLA_V), lambda h, s, ii, jj: (ii[s], h)),
        scratch_shapes=[pltpu.VMEM((hp, t, LANE), F32), pltpu.VMEM((hp, t, LANE), F32),
                        pltpu.VMEM((hp, t, MLA_V), F32)],
    )
    return pl.pallas_call(
        _mla_prompt_kernel, grid_spec=grid_spec, out_shape=jax.ShapeDtypeStruct((rows, WIDTH_A), F32),
        compiler_params=_params(("arbitrary", "arbitrary")),
    )(ii, jj, qa, kn, va)


def _moba_prompt_kernel(ii_ref, jj_ref, q_ref, b_ref, k_ref, v_ref, o_ref, m_scr, l_scr, acc_scr):
    t = pl.program_id(0)
    i = ii_ref[t]
    j = jj_ref[t]
    blk = MOBA_BLOCK
    nrow = MOBA_REP * blk
    scratch = [(m_scr.at[g], l_scr.at[g], acc_scr.at[g]) for g in range(MOBA_KV_HEADS)]

    @pl.when(j == 0)
    def _():
        _flash_init(scratch)

    onehot = jnp.where(lax.broadcasted_iota(jnp.int32, (blk, LANE), 1) == j, 1.0, 0.0).astype(BF16)
    qpos = lax.broadcasted_iota(jnp.int32, (MOBA_REP, blk, blk), 1).reshape(nrow, blk)
    kpos = lax.broadcasted_iota(jnp.int32, (nrow, blk), 1)
    visible = (j < i) | (kpos <= qpos)
    scores, values = [], []
    for g in range(MOBA_KV_HEADS):
        heads = slice(g * MOBA_REP, (g + 1) * MOBA_REP)
        q_aug = jnp.concatenate([q_ref[heads].reshape(nrow, MOBA_DIM), b_ref[heads].reshape(nrow, LANE)], axis=-1)
        k_aug = jnp.concatenate([k_ref[:, g * MOBA_DIM:(g + 1) * MOBA_DIM], onehot], axis=-1)
        scores.append(jnp.where(visible, _mm_nt(q_aug, k_aug), NEG))
        values.append(v_ref[:, g * MOBA_DIM:(g + 1) * MOBA_DIM])
    _flash_update(scores, values, scratch)

    @pl.when(j == i)
    def _():
        for g in range(MOBA_KV_HEADS):
            o = acc_scr[g] / l_scr[g]
            for r in range(MOBA_REP):
                h = g * MOBA_REP + r
                o_ref[:, h * MOBA_DIM:(h + 1) * MOBA_DIM] = o[r * blk:(r + 1) * blk]


def _moba_prompt(qb16, bias, kb16, vb16):
    rows = qb16.shape[1]
    nb = rows // MOBA_BLOCK
    ii, jj = _tri_pairs(nb)
    nrow = MOBA_REP * MOBA_BLOCK
    kvw = MOBA_KV_HEADS * MOBA_DIM
    grid_spec = pltpu.PrefetchScalarGridSpec(
        num_scalar_prefetch=2, grid=(ii.shape[0],),
        in_specs=[
            pl.BlockSpec((MOBA_HEADS, MOBA_BLOCK, MOBA_DIM), lambda s, ii, jj: (0, ii[s], 0)),
            pl.BlockSpec((MOBA_HEADS, MOBA_BLOCK, LANE), lambda s, ii, jj: (0, ii[s], 0)),
            pl.BlockSpec((MOBA_BLOCK, kvw), lambda s, ii, jj: (jj[s], 0)),
            pl.BlockSpec((MOBA_BLOCK, kvw), lambda s, ii, jj: (jj[s], 0)),
        ],
        out_specs=pl.BlockSpec((MOBA_BLOCK, WIDTH_B), lambda s, ii, jj: (ii[s], 0)),
        scratch_shapes=[pltpu.VMEM((MOBA_KV_HEADS, nrow, LANE), F32), pltpu.VMEM((MOBA_KV_HEADS, nrow, LANE), F32),
                        pltpu.VMEM((MOBA_KV_HEADS, nrow, MOBA_DIM), F32)],
    )
    return pl.pallas_call(
        _moba_prompt_kernel, grid_spec=grid_spec, out_shape=jax.ShapeDtypeStruct((rows, WIDTH_B), F32),
        compiler_params=_params(("arbitrary",)),
    )(ii, jj, qb16, bias, kb16, vb16)


PAGES_PER_STEP = 16
COMBINE_UNROLL = 8


def _paged_copies(pt_ref, srcs, dsts, sem, b, j, slot):
    copies = []
    for p in range(PAGES_PER_STEP):
        page = pt_ref[b, j * PAGES_PER_STEP + p]
        for a, (src, dst) in enumerate(zip(srcs, dsts)):
            copies.append((pltpu.make_async_copy(src.at[page], dst(slot, p), sem.at[a, slot]), min(a, 1)))
    return copies


def _paged_pipeline(pt_ref, srcs, dsts, sem):
    b = pl.program_id(0)
    j = pl.program_id(1)
    nj = pl.num_programs(1)
    step = b * nj + j
    slot = step % 2

    @pl.when(step == 0)
    def _():
        for cp, prio in _paged_copies(pt_ref, srcs, dsts, sem, b, j, slot):
            cp.start(priority=prio)

    @pl.when(step + 1 < pl.num_programs(0) * nj)
    def _():
        last = j + 1 == nj
        nb_ = jnp.where(last, b + 1, b)
        nj_ = jnp.where(last, 0, j + 1)
        for cp, prio in _paged_copies(pt_ref, srcs, dsts, sem, nb_, nj_, 1 - slot):
            cp.start(priority=prio)

    for cp, _ in _paged_copies(pt_ref, srcs, dsts, sem, b, j, slot):
        cp.wait()
    return slot


def _mla_paged_step(slot, between, qbd_ref, qpe_ref, wukt_ref, wuv_ref, cnew_ref, knew_ref,
                    o_ref, lat_buf, kpe_buf, lhs_scr, m_scr, l_scr, ctx_scr):
    b = pl.program_id(0)
    j = pl.program_id(1)
    nj = pl.num_programs(1)
    nq = qbd_ref.shape[1]
    nhd = MLA_HEADS * MLA_NOPE
    scratch = [(m_scr, l_scr, ctx_scr)]

    @pl.when((b == 0) & (j == 0))
    def _():
        lhs_scr[:nhd, :] = wukt_ref[...]

    @pl.when(j == 0)
    def _():
        lhs_scr[nhd:, :] = _mm(qbd_ref[0], wukt_ref[...]).astype(BF16)
        _flash_init(scratch)

    def project(c):
        cb = c.astype(BF16)
        return cb, _mm_nt(lhs_scr[...], cb)

    def attend(cb, res, kpe_t, mask):
        n = cb.shape[0]
        kn = res[:nhd]
        n2 = jnp.sum((kn * kn).reshape(MLA_HEADS, MLA_NOPE, n), axis=1)
        pe2 = jnp.sum(kpe_t * kpe_t, axis=0, keepdims=True)
        spe = _mm(qpe_ref[0], kpe_t.astype(BF16))
        r = lax.rsqrt((n2 + pe2) * (1.0 / MLA_QK) + EPS)
        s = ((res[nhd:] + spe).reshape(nq // MLA_HEADS, MLA_HEADS, n) * r[None]).reshape(nq, n)
        if mask is not None:
            s = jnp.where(mask, s, NEG)
        _flash_update([s], [cb], scratch)

    cb, res = project(lat_buf[slot])
    between()
    attend(cb, res, jnp.concatenate([kpe_buf[slot, p] for p in range(PAGES_PER_STEP)], axis=-1), None)

    @pl.when(j == nj - 1)
    def _():
        nnew = cnew_ref.shape[1]
        qidx = lax.broadcasted_iota(jnp.int32, (nq // MLA_HEADS, MLA_HEADS, nnew), 0).reshape(nq, nnew)
        kidx = lax.broadcasted_iota(jnp.int32, (nq, nnew), 1)
        cb, res = project(cnew_ref[0])
        attend(cb, res, knew_ref[0], kidx <= qidx)
        ctx = (ctx_scr[...] / _rep(l_scr[...], MLA_KV_LORA)).astype(BF16)
        full = _mm(ctx, wuv_ref[...])
        hrow = lax.broadcasted_iota(jnp.int32, (nq // MLA_HEADS, MLA_HEADS, MLA_V), 1).reshape(nq, MLA_V)
        out = jnp.zeros((nq, MLA_V), F32)
        for h in range(MLA_HEADS):
            out = out + jnp.where(hrow == h, full[:, h * MLA_V:(h + 1) * MLA_V], 0.0)
        o_ref[0] = out


def _moba_paged_step(slot, q16_ref, qf_ref, knew_ref, vnew_ref, o_ref, k_buf, v_buf, m_s, l_s, g_s, o_s,
                     *, page_size, nb):
    j = pl.program_id(1)
    nj = pl.num_programs(1)
    nq = q16_ref.shape[1]
    blocks_per_step = PAGES_PER_STEP * page_size // MOBA_BLOCK

    def block(buf, u):
        base = u * MOBA_KV_HEADS * MOBA_BLOCK
        return jnp.concatenate(
            [buf[slot, pl.ds(base + g, MOBA_BLOCK, stride=MOBA_KV_HEADS), :] for g in range(MOBA_KV_HEADS)], axis=-1)

    q16 = q16_ref[0]
    qf = qf_ref[0]
    wide = (nq, LANE)
    k16 = []
    for u in range(blocks_per_step):
        kf = block(k_buf, u)
        kmean = jnp.mean(kf, axis=0, keepdims=True)
        g_s[j * blocks_per_step + u] = jnp.broadcast_to(jnp.sum(qf * kmean, axis=-1, keepdims=True), wide)
        k16.append(kf.astype(BF16))
    s_all = _mm_nt(q16, jnp.concatenate(k16, axis=0))
    probs = []
    for u in range(blocks_per_step):
        jb = j * blocks_per_step + u
        s = s_all[:, u * MOBA_BLOCK:(u + 1) * MOBA_BLOCK]
        m = jnp.max(s, axis=-1, keepdims=True)
        p = jnp.exp(s - m)
        m_s[jb] = jnp.broadcast_to(m, wide)
        l_s[jb] = jnp.broadcast_to(jnp.sum(p, axis=-1, keepdims=True), wide)
        probs.append(p.astype(BF16))
    for u in range(blocks_per_step):
        o_s[j * blocks_per_step + u] = _mm(probs[u], block(v_buf, u).astype(BF16))

    @pl.when(j == nj - 1)
    def _():
        lane = lax.broadcasted_iota(jnp.int32, wide, 1)

        def gather(jb, carry):
            gm, mm, lm = carry
            hit = lane == jb
            return jnp.where(hit, g_s[jb], gm), jnp.where(hit, m_s[jb], mm), jnp.where(hit, l_s[jb], lm)
        gm, mm, lm = lax.fori_loop(
            0, nb, gather, (jnp.full(wide, -jnp.inf, F32), jnp.full(wide, NEG, F32), jnp.zeros(wide, F32)),
            unroll=COMBINE_UNROLL)

        def count(o, rank):
            other = g_s[o]
            ahead = (other > gm) | ((other == gm) & (o < lane))
            return rank + jnp.where(ahead, 1.0, 0.0)
        rank = lax.fori_loop(0, nb, count, jnp.zeros(wide, F32), unroll=COMBINE_UNROLL)
        keep = (rank < MOBA_TOPK) & (lane < nb)

        nnew = knew_ref.shape[1]
        qidx = lax.broadcasted_iota(jnp.int32, (nq // MOBA_HEADS, MOBA_HEADS, nnew), 0).reshape(nq, nnew)
        kidx = lax.broadcasted_iota(jnp.int32, (nq, nnew), 1)
        s_own = jnp.where(kidx <= qidx, _mm_nt(q16, knew_ref[0].astype(BF16)), NEG)
        m_own = jnp.max(s_own, axis=-1, keepdims=True)
        m_tot = jnp.maximum(jnp.max(jnp.where(keep, mm, NEG), axis=-1, keepdims=True), m_own)
        p_own = jnp.exp(s_own - m_tot)
        wgt = jnp.exp(jnp.where(keep, mm - m_tot, NEG))
        l_tot = jnp.sum(wgt * lm, axis=-1, keepdims=True) + jnp.sum(p_own, axis=-1, keepdims=True)
        acc0 = _mm(p_own.astype(BF16), vnew_ref[0].astype(BF16))

        def combine(jb, acc):
            w_col = jnp.sum(jnp.where(lane == jb, wgt, 0.0), axis=-1, keepdims=True)
            return acc + w_col * o_s[jb]
        acc = lax.fori_loop(0, nb, combine, acc0, unroll=COMBINE_UNROLL) / l_tot
        grow = lax.broadcasted_iota(jnp.int32, (nq // MOBA_HEADS, MOBA_HEADS, MOBA_DIM), 1).reshape(nq, MOBA_DIM)
        o_ref[0] = jnp.where(grow < MOBA_REP, acc[:, :MOBA_DIM], acc[:, MOBA_DIM:])


def _sample_attn_kernel(pt_ref, qbd_ref, qpe_ref, wukt_ref, wuv_ref, cnew_ref, cknew_ref, q16_ref, qf_ref,
                        knew_ref, vnew_ref, lat_hbm, kpe_hbm, k_hbm, v_hbm, oa_ref, ob_ref,
                        lat_buf, kpe_buf, k_buf, v_buf, sem, lhs_scr, m_scr, l_scr, ctx_scr, m_s, l_s, g_s, o_s,
                        *, page_size, nb):
    page_rows = MOBA_KV_HEADS * page_size
    window = lambda buf: (lambda s, p: buf.at[s, pl.ds(p * page_rows, page_rows)])
    slot = _paged_pipeline(
        pt_ref, (lat_hbm, kpe_hbm, k_hbm, v_hbm),
        (lambda s, p: lat_buf.at[s, pl.ds(p * page_size, page_size)], lambda s, p: kpe_buf.at[s, p],
         window(k_buf), window(v_buf)), sem)
    moba = functools.partial(_moba_paged_step, slot, q16_ref, qf_ref, knew_ref, vnew_ref, ob_ref, k_buf, v_buf,
                             m_s, l_s, g_s, o_s, page_size=page_size, nb=nb)
    _mla_paged_step(slot, lambda: None, qbd_ref, qpe_ref, wukt_ref, wuv_ref, cnew_ref, cknew_ref, oa_ref,
                    lat_buf, kpe_buf, lhs_scr, m_scr, l_scr, ctx_scr)
    moba()


def _sample_attention(page_table, qbd, qpe, wukt, wuv, cnew, cknew_t, q16, qf, knew, vnew,
                      cache_lat, cache_kpe_t, cache_k, cache_v):
    ns, n_pages = page_table.shape
    page_size = cache_lat.shape[1]
    past = n_pages * page_size
    assert n_pages % PAGES_PER_STEP == 0 and past % MOBA_BLOCK == 0
    assert (PAGES_PER_STEP * page_size) % MOBA_BLOCK == 0
    nb = past // MOBA_BLOCK
    assert nb <= LANE
    nq = qbd.shape[1]
    keys = PAGES_PER_STEP * page_size
    nhd = MLA_HEADS * MLA_NOPE
    kvw = MOBA_KV_HEADS * MOBA_DIM
    per_seq = lambda shape: pl.BlockSpec((1,) + shape, lambda b, j, pt: (b, 0, 0))
    hbm = pl.BlockSpec(memory_space=pl.ANY)
    grid_spec = pltpu.PrefetchScalarGridSpec(
        num_scalar_prefetch=1, grid=(ns, n_pages // PAGES_PER_STEP),
        in_specs=[
            per_seq((nq, nhd)), per_seq((nq, MLA_ROPE)),
            _const_spec((nhd, MLA_KV_LORA)), _const_spec((MLA_KV_LORA, MLA_HEADS * MLA_V)),
            per_seq(cnew.shape[1:]), per_seq(cknew_t.shape[1:]),
            per_seq((nq, kvw)), per_seq((nq, kvw)), per_seq(knew.shape[1:]), per_seq(vnew.shape[1:]),
            hbm, hbm, hbm, hbm,
        ],
        out_specs=(per_seq((nq, MLA_V)), per_seq((nq, MOBA_DIM))),
        scratch_shapes=[
            pltpu.VMEM((2, keys, MLA_KV_LORA), F32), pltpu.VMEM((2, PAGES_PER_STEP, MLA_ROPE, page_size), F32),
            pltpu.VMEM((2, keys * MOBA_KV_HEADS, MOBA_DIM), F32), pltpu.VMEM((2, keys * MOBA_KV_HEADS, MOBA_DIM), F32),
            pltpu.SemaphoreType.DMA((4, 2)),
            pltpu.VMEM((nhd + nq, MLA_KV_LORA), BF16),
            pltpu.VMEM((nq, LANE), F32), pltpu.VMEM((nq, LANE), F32), pltpu.VMEM((nq, MLA_KV_LORA), F32),
            pltpu.VMEM((nb, nq, LANE), F32), pltpu.VMEM((nb, nq, LANE), F32), pltpu.VMEM((nb, nq, LANE), F32),
            pltpu.VMEM((nb, nq, kvw), F32),
        ],
    )
    return pl.pallas_call(
        functools.partial(_sample_attn_kernel, page_size=page_size, nb=nb), grid_spec=grid_spec,
        out_shape=(jax.ShapeDtypeStruct((ns, nq, MLA_V), F32), jax.ShapeDtypeStruct((ns, nq, MOBA_DIM), F32)),
        compiler_params=_params(("arbitrary", "arbitrary")),
    )(page_table, qbd, qpe, wukt, wuv, cnew, cknew_t, q16, qf, knew, vnew, cache_lat, cache_kpe_t, cache_k, cache_v)


def _merge_kernel(x_ref, oa_ref, ob_ref, sga_ref, sgb_ref, w_ref, o_ref):
    mix = jnp.concatenate([(oa_ref[...] * sga_ref[...].astype(F32)).astype(BF16),
                           (ob_ref[...] * sgb_ref[...].astype(F32)).astype(BF16)], axis=-1)
    o_ref[...] = x_ref[...] + _mm(mix, w_ref[...])


def _even_merge(x, oa, ob, sga, sgb, w_out):
    rows = x.shape[0]
    tm = min(256, rows)
    assert rows % tm == 0
    row = lambda width: pl.BlockSpec((tm, width), lambda i: (i, 0))
    return pl.pallas_call(
        _merge_kernel, grid=(rows // tm,),
        in_specs=[row(D_MODEL), row(WIDTH_A), row(WIDTH_B), row(WIDTH_A), row(WIDTH_B),
                  _const_spec((WIDTH_A + WIDTH_B, D_MODEL))],
        out_specs=row(D_MODEL), out_shape=jax.ShapeDtypeStruct((rows, D_MODEL), F32),
        compiler_params=_params(("arbitrary",)),
    )(x, oa, ob, sga, sgb, w_out)


def _conv_kernel(x_ref, norm_ref, wb_ref, wc_ref, wh_ref, wg_ref, cw_ref, wo_ref, prev_ref,
                 o_ref, st_ref, xn_scr, acc_scr, carry_scr, *, step_major_seqs):
    i = pl.program_id(0)
    n = pl.program_id(1)
    tm = x_ref.shape[0]

    @pl.when(n == 0)
    def _():
        x = x_ref[...]
        xn_scr[...] = ((x * _rms(x, D_MODEL)) * norm_ref[...]).astype(BF16)
        acc_scr[...] = x

    xn = xn_scr[...]
    ch = _mm(xn, wc_ref[...]) * _mm(xn, wh_ref[...])
    cw = cw_ref[...]
    if step_major_seqs:
        ns = step_major_seqs
        full = jnp.concatenate([prev_ref[...], ch], axis=0)
        conv = cw[0:1] * full[0:tm] + cw[1:2] * full[ns:ns + tm] + cw[2:3] * full[2 * ns:2 * ns + tm]
        st_ref[...] = ch[tm - 2 * ns:]
    else:
        @pl.when(i == 0)
        def _():
            carry_scr[n] = jnp.zeros(carry_scr.shape[1:], F32)

        prev = carry_scr[n]
        rowi = lax.broadcasted_iota(jnp.int32, ch.shape, 0)
        m1 = jnp.where(rowi == 0, prev[SUBLANE - 1:SUBLANE], pltpu.roll(ch, 1, 0))
        m2 = jnp.where(rowi == 0, prev[SUBLANE - 2:SUBLANE - 1],
                       jnp.where(rowi == 1, prev[SUBLANE - 1:SUBLANE], pltpu.roll(ch, 2, 0)))
        conv = cw[0:1] * m2 + cw[1:2] * m1 + cw[2:3] * ch
        carry_scr[n] = ch[tm - SUBLANE:]
        st_ref[...] = ch[tm - SUBLANE:]
    y = (_mm(xn, wb_ref[...]) * conv * _silu(_mm(xn, wg_ref[...]))).astype(BF16)
    acc_scr[...] += _mm(y, wo_ref[...])

    @pl.when(n == pl.num_programs(1) - 1)
    def _():
        o_ref[...] = acc_scr[...]


def _conv_layer(x, prev, w, step_major_seqs):
    rows = x.shape[0]
    tm = rows if step_major_seqs else min(512, rows)
    tn = 512
    assert rows % tm == 0
    ncol = CONV_DIM // tn
    st_rows = prev.shape[0] if step_major_seqs else SUBLANE
    wcol = lambda k: pl.BlockSpec((D_MODEL, tn), lambda i, n, k=k: (0, n + k * ncol))
    return pl.pallas_call(
        functools.partial(_conv_kernel, step_major_seqs=step_major_seqs), grid=(rows // tm, ncol),
        in_specs=[
            pl.BlockSpec((tm, D_MODEL), lambda i, n: (i, 0)), _const_spec((1, D_MODEL)),
            wcol(0), wcol(1), wcol(2), wcol(3),
            pl.BlockSpec((CONV_WIDTH, tn), lambda i, n: (0, n)),
            pl.BlockSpec((tn, D_MODEL), lambda i, n: (n, 0)),
            pl.BlockSpec((prev.shape[0], tn), lambda i, n: (0, n)),
        ],
        out_specs=(pl.BlockSpec((tm, D_MODEL), lambda i, n: (i, 0)),
                   pl.BlockSpec((st_rows, tn), lambda i, n: (i, n))),
        out_shape=(jax.ShapeDtypeStruct((rows, D_MODEL), F32),
                   jax.ShapeDtypeStruct((rows // tm * st_rows, CONV_DIM), F32)),
        scratch_shapes=[pltpu.VMEM((tm, D_MODEL), BF16), pltpu.VMEM((tm, D_MODEL), F32),
                        pltpu.VMEM((ncol, SUBLANE, tn), F32)],
        compiler_params=_params(("arbitrary", "arbitrary")),
    )(x, w["norm"], w["w_in"], w["w_in"], w["w_in"], w["w_in"], w["conv_w"], w["w_out"], prev)


def _rot_cols(w):
    half = MLA_ROPE // 2
    return jnp.concatenate([-w[..., half:], w[..., :half]], axis=-1)


def _pad_cols(w, width):
    return jnp.pad(w, ((0, 0), (0, width - w.shape[-1])))


def _gain_pad(g):
    full = jnp.concatenate([g[:MLA_NOPE], g[MLA_NOPE:], g[MLA_NOPE:]])
    return jnp.pad(full, (0, MLA_QK_PAD - MLA_QK)).reshape(1, MLA_QK_PAD)


def _even_weights(norm, w_in, qa_norm, w_uq, kva_norm, w_uk, w_uv, q_norm, k_norm, moba_qn, moba_kn):
    o = np.cumsum((0, MLA_Q_LORA, MLA_KV_LORA, MLA_ROPE, WIDTH_B, MOBA_KV_HEADS * MOBA_DIM,
                   MOBA_KV_HEADS * MOBA_DIM, WIDTH_A, WIDTH_B))
    kpe_w = w_in[:, o[2]:o[3]]
    w_in2 = jnp.concatenate([
        w_in[:, o[0]:o[2]], _pad_cols(kpe_w, LANE), _pad_cols(_rot_cols(kpe_w), LANE), w_in[:, o[3]:],
    ], axis=-1).astype(BF16)
    uq = w_uq.reshape(MLA_Q_LORA, MLA_HEADS, MLA_QK)
    pe_w = uq[..., MLA_NOPE:]
    pad = jnp.zeros((MLA_Q_LORA, MLA_HEADS, LANE - MLA_ROPE), F32)
    uq2 = jnp.concatenate([uq[..., :MLA_NOPE], pe_w, pad, _rot_cols(pe_w), pad], axis=-1)
    return {
        "norm": norm.reshape(1, D_MODEL), "w_in": w_in2, "qa_norm": qa_norm.reshape(1, MLA_Q_LORA),
        "w_uq": uq2.reshape(MLA_Q_LORA, MLA_HEADS * UQ_HEAD_COLS).astype(BF16),
        "kva_norm": kva_norm.reshape(1, MLA_KV_LORA), "q_gain": _gain_pad(q_norm), "k_gain": _gain_pad(k_norm),
        "moba_qn": moba_qn.reshape(1, MOBA_DIM), "moba_kn": moba_kn.reshape(1, MOBA_DIM),
        "w_ukv": jnp.concatenate([w_uk, w_uv], axis=-1).astype(BF16),
    }


def _rope_tables(pos):
    half = MLA_ROPE // 2
    inv = ROPE_THETA ** (-jnp.arange(half, dtype=F32) / half)
    ang = pos.astype(F32)[:, None] * inv
    cos, sin = jnp.cos(ang), jnp.sin(ang)
    pad = jnp.zeros((pos.shape[0], LANE - MLA_ROPE), F32)
    return jnp.concatenate([cos, cos, pad], axis=-1), jnp.concatenate([sin, sin, pad], axis=-1)


def _block_diag_rows(x, nblk):
    eye = jnp.eye(nblk, dtype=x.dtype)
    out = x[..., :, None, :] * eye[:, :, None]
    return out.reshape(x.shape[:-1] + (nblk * x.shape[-1],))


def kernel(x_prompt, x_sample, cache_mla_latent, cache_mla_kpe, cache_moba_k, cache_moba_v, state_conv, page_table, norm_even, w_in_even, mla_qa_norm, mla_w_uq, mla_kva_norm, mla_w_uk, mla_w_uv, mla_q_norm, mla_k_norm, moba_q_norm, moba_k_norm, w_out_even, norm_odd, w_in_odd, conv_w, w_out_odd):
    assert x_prompt.shape[0] == 1 and norm_even.shape[0] == 1 and norm_odd.shape[0] == 1
    seq = x_prompt.shape[1]
    ns, steps = x_sample.shape[:2]
    n_pages = page_table.shape[1]
    page_size = cache_mla_latent.shape[2]
    past = n_pages * page_size
    assert seq % MOBA_BLOCK == 0 and steps <= SUBLANE
    kvw = MOBA_KV_HEADS * MOBA_DIM

    we = _even_weights(norm_even[0], w_in_even[0], mla_qa_norm[0], mla_w_uq[0], mla_kva_norm[0], mla_w_uk[0],
                       mla_w_uv[0], mla_q_norm[0], mla_k_norm[0], moba_q_norm[0], moba_k_norm[0])
    w_out_e = w_out_even[0].astype(BF16)
    wo = {"norm": norm_odd[0].reshape(1, D_MODEL), "w_in": w_in_odd[0].astype(BF16), "conv_w": conv_w[0],
          "w_out": w_out_odd[0].astype(BF16)}

    xp = x_prompt[0]
    cs, sn = _rope_tables(jnp.arange(seq))
    (qa, kn, va, lat_p, kpe_p, qb16, qbf, kb_p, vb_p, kb16, vb16, kmean, sga, sgb) = _even_project(xp, cs, sn, we, BF16)
    oa = _mla_prompt(qa, kn, va)
    bias = _moba_gate(qbf, kmean, seq // MOBA_BLOCK)
    ob = _moba_prompt(qb16, bias, kb16, vb16)
    y1p = _even_merge(xp, oa, ob, sga, sgb, w_out_e)
    y2p, st_p = _conv_layer(y1p, jnp.zeros((SUBLANE, CONV_DIM), F32), wo, 0)

    rows_s = steps * ns
    xs = x_sample.transpose(1, 0, 2).reshape(rows_s, D_MODEL)
    cs_s, sn_s = _rope_tables(past + jnp.repeat(jnp.arange(steps), ns))
    (qa_s, _, _, lat_s, kpe_s, qb16_s, qbf_s, kb_s, vb_s, _, _, _, sga_s, sgb_s) = _even_project(xs, cs_s, sn_s, we, F32)

    def per_seq(a):
        return a.reshape(steps, ns, a.shape[-1]).transpose(1, 0, 2)

    def pad_new(a):
        return jnp.pad(a, ((0, 0), (0, LANE - steps), (0, 0)))

    nq = steps * MLA_HEADS
    qk = qa_s.reshape(MLA_HEADS, steps, ns, MLA_QK_PAD).transpose(2, 1, 0, 3) * we["k_gain"][0]
    qbd = _block_diag_rows(qk[..., :MLA_NOPE], MLA_HEADS).reshape(ns, nq, MLA_HEADS * MLA_NOPE).astype(BF16)
    qpe = qk[..., MLA_NOPE:MLA_QK].reshape(ns, nq, MLA_ROPE).astype(BF16)

    def group_diag(a):
        g = a.reshape(ns, steps, MOBA_KV_HEADS, MOBA_REP, MOBA_DIM).transpose(0, 1, 3, 2, 4)
        return _block_diag_rows(g, MOBA_KV_HEADS).transpose(0, 1, 3, 2, 4).reshape(ns, nq, kvw)

    q16_s = group_diag(qb16_s.reshape(MOBA_HEADS, steps, ns, MOBA_DIM).transpose(2, 1, 0, 3))
    qf_s = group_diag(per_seq(qbf_s).reshape(ns, steps, MOBA_HEADS, MOBA_DIM))
    oa_s, ob_s = _sample_attention(
        page_table, qbd, qpe, mla_w_uk[0].T.astype(BF16), mla_w_uv[0].astype(BF16),
        pad_new(per_seq(lat_s)), pad_new(per_seq(kpe_s)).transpose(0, 2, 1),
        q16_s, qf_s, pad_new(per_seq(kb_s)), pad_new(per_seq(vb_s)),
        cache_mla_latent[0], cache_mla_kpe[0].transpose(0, 2, 1),
        cache_moba_k[0].reshape(-1, page_size * MOBA_KV_HEADS, MOBA_DIM),
        cache_moba_v[0].reshape(-1, page_size * MOBA_KV_HEADS, MOBA_DIM))

    def step_major(o):
        return o.reshape(ns, steps, -1).transpose(1, 0, 2).reshape(rows_s, -1)

    y1s = _even_merge(xs, step_major(oa_s), step_major(ob_s), sga_s, sgb_s, w_out_e)
    prev_s = state_conv[0].transpose(1, 0, 2).reshape((CONV_WIDTH - 1) * ns, CONV_DIM)
    y2s, st_s = _conv_layer(y1s, prev_s, wo, ns)

    return (
        y2p[None], per_seq(y2s),
        lat_p[None, None], kpe_p[None, None],
        kb_p.reshape(1, 1, seq, MOBA_KV_HEADS, MOBA_DIM), vb_p.reshape(1, 1, seq, MOBA_KV_HEADS, MOBA_DIM),
        st_p[st_p.shape[0] - (CONV_WIDTH - 1):][None, None],
        per_seq(lat_s)[None], per_seq(kpe_s)[None],
        per_seq(kb_s).reshape(1, ns, steps, MOBA_KV_HEADS, MOBA_DIM),
        per_seq(vb_s).reshape(1, ns, steps, MOBA_KV_HEADS, MOBA_DIM),
        st_s.reshape(CONV_WIDTH - 1, ns, CONV_DIM).transpose(1, 0, 2)[None],
    )
```

```python
import functools

import jax
import jax.numpy as jnp
import numpy as np
from jax import lax
from jax.experimental import pallas as pl
from jax.experimental.pallas import tpu as pltpu

F32 = jnp.float32
BF16 = jnp.bfloat16

D_MODEL = 2048
MLA_HEADS = 8
MLA_NOPE = 128
MLA_ROPE = 64
MLA_QK = MLA_NOPE + MLA_ROPE
MLA_QK_PAD = 256
MLA_V = 128
MLA_Q_LORA = 512
MLA_KV_LORA = 512
ROPE_THETA = 10000.0
MOBA_HEADS = 8
MOBA_KV_HEADS = 2
MOBA_REP = MOBA_HEADS // MOBA_KV_HEADS
MOBA_DIM = 128
MOBA_BLOCK = 256
MOBA_TOPK = 3
CONV_DIM = D_MODEL
CONV_WIDTH = 3
WIDTH_A = MLA_HEADS * MLA_V
WIDTH_B = MOBA_HEADS * MOBA_DIM
EPS = 1e-6
NEG = -1e30
LOG2E = 1.4426950408889634
LANE = 128
SUBLANE = 8
VMEM_LIMIT = 56 * 1024 * 1024

O_QLAT = 0
O_C = O_QLAT + MLA_Q_LORA
O_KPE = O_C + MLA_KV_LORA
O_QB = O_KPE + 2 * LANE
O_KB = O_QB + WIDTH_B
O_VB = O_KB + MOBA_KV_HEADS * MOBA_DIM
O_GA = O_VB + MOBA_KV_HEADS * MOBA_DIM
O_GB = O_GA + WIDTH_A
EVEN_COLS = O_GB + WIDTH_B
UQ_HEAD_COLS = 3 * LANE


def _mm(a, b):
    return jnp.dot(a, b, preferred_element_type=F32)


def _mm_nt(a, b, precision=None):
    return lax.dot_general(a, b, (((1,), (1,)), ((), ())), preferred_element_type=F32, precision=precision)


def _const_spec(shape):
    return pl.BlockSpec(shape, lambda *_: (0,) * len(shape), pipeline_mode=pl.Buffered(1))


def _params(sem):
    return pltpu.CompilerParams(dimension_semantics=sem, vmem_limit_bytes=VMEM_LIMIT)


def _rms(x, width):
    return lax.rsqrt(jnp.sum(x * x, axis=-1, keepdims=True) * (1.0 / width) + EPS)


def _silu(x):
    return x * jax.nn.sigmoid(x)


def _proj_kernel(x_ref, cs_ref, sn_ref, norm_ref, win_ref, qan_ref, wuq_ref, kvan_ref, qg_ref, kg_ref,
                 mqn_ref, mkn_ref, wukv_ref,
                 qa_ref, kn_ref, va_ref, lat_ref, kpe_ref, qb16_ref, qbf_ref, kb_ref, vb_ref, kb16_ref,
                 vb16_ref, kmean_ref, sga_ref, sgb_ref):
    x = x_ref[...]
    xn = ((x * _rms(x, D_MODEL)) * norm_ref[...]).astype(BF16)
    cs = cs_ref[...]
    sn = sn_ref[...]

    ql = _mm(xn, win_ref[:, O_QLAT:O_QLAT + MLA_Q_LORA])
    qln = ((ql * _rms(ql, MLA_Q_LORA)) * qan_ref[...]).astype(BF16)
    qh_all = _mm(qln, wuq_ref[...])
    qg = qg_ref[...]
    q_scale = MLA_QK ** -0.5 * LOG2E
    for h in range(MLA_HEADS):
        base = h * UQ_HEAD_COLS
        nope = qh_all[:, base:base + LANE]
        pe = qh_all[:, base + LANE:base + 2 * LANE] * cs + qh_all[:, base + 2 * LANE:base + 3 * LANE] * sn
        ssq = jnp.sum(nope * nope, axis=-1, keepdims=True) + jnp.sum(pe * pe, axis=-1, keepdims=True)
        rr = lax.rsqrt(ssq * (1.0 / MLA_QK) + EPS) * q_scale
        qa_ref[h] = jnp.concatenate([nope * rr * qg[:, :LANE], pe * rr * qg[:, LANE:]], axis=-1).astype(qa_ref.dtype)

    cz = _mm(xn, win_ref[:, O_C:O_C + MLA_KV_LORA])
    c = (cz * _rms(cz, MLA_KV_LORA)) * kvan_ref[...]
    lat_ref[...] = c
    kp = _mm(xn, win_ref[:, O_KPE:O_KPE + 2 * LANE])
    kpe_t = kp[:, :LANE] * cs + kp[:, LANE:] * sn
    kpe_ref[...] = kpe_t[:, :MLA_ROPE]
    kv = _mm(c.astype(BF16), wukv_ref[...])
    va_ref[...] = kv[:, MLA_HEADS * MLA_NOPE:].astype(BF16)
    pes = jnp.sum(kpe_t * kpe_t, axis=-1, keepdims=True)
    kg = kg_ref[...]
    for h in range(MLA_HEADS):
        kh = kv[:, h * MLA_NOPE:(h + 1) * MLA_NOPE]
        rr = lax.rsqrt((jnp.sum(kh * kh, axis=-1, keepdims=True) + pes) * (1.0 / MLA_QK) + EPS)
        kn_ref[h] = jnp.concatenate([kh * rr * kg[:, :LANE], kpe_t * rr * kg[:, LANE:]], axis=-1).astype(BF16)

    qb = _mm(xn, win_ref[:, O_QB:O_QB + WIDTH_B])
    m_scale = MOBA_DIM ** -0.5 * LOG2E
    for h in range(MOBA_HEADS):
        t = qb[:, h * MOBA_DIM:(h + 1) * MOBA_DIM]
        t = (t * _rms(t, MOBA_DIM)) * mqn_ref[...]
        qbf_ref[:, h * MOBA_DIM:(h + 1) * MOBA_DIM] = t
        qb16_ref[h] = (t * m_scale).astype(BF16)
    kbz = _mm(xn, win_ref[:, O_KB:O_KB + MOBA_KV_HEADS * MOBA_DIM])
    kparts = []
    for g in range(MOBA_KV_HEADS):
        t = kbz[:, g * MOBA_DIM:(g + 1) * MOBA_DIM]
        kparts.append((t * _rms(t, MOBA_DIM)) * mkn_ref[...])
    kb = jnp.concatenate(kparts, axis=-1)
    kb_ref[...] = kb
    kb16_ref[...] = kb.astype(BF16)
    kmean_ref[0] = jnp.mean(kb, axis=0, keepdims=True)
    vb = _mm(xn, win_ref[:, O_VB:O_VB + MOBA_KV_HEADS * MOBA_DIM])
    vb_ref[...] = vb
    vb16_ref[...] = vb.astype(BF16)

    sga_ref[...] = _silu(_mm(xn, win_ref[:, O_GA:O_GA + WIDTH_A])).astype(BF16)
    sgb_ref[...] = _silu(_mm(xn, win_ref[:, O_GB:O_GB + WIDTH_B])).astype(BF16)


def _even_project(x, cs, sn, w, q_dtype):
    rows = x.shape[0]
    tm = min(MOBA_BLOCK, rows)
    assert rows % tm == 0
    nt = rows // tm
    row = lambda width: pl.BlockSpec((tm, width), lambda i: (i, 0))
    head = lambda width: pl.BlockSpec((MLA_HEADS, tm, width), lambda i: (0, i, 0))
    kvw = MOBA_KV_HEADS * MOBA_DIM
    out_shape = (
        jax.ShapeDtypeStruct((MLA_HEADS, rows, MLA_QK_PAD), q_dtype),
        jax.ShapeDtypeStruct((MLA_HEADS, rows, MLA_QK_PAD), BF16),
        jax.ShapeDtypeStruct((rows, WIDTH_A), BF16),
        jax.ShapeDtypeStruct((rows, MLA_KV_LORA), F32),
        jax.ShapeDtypeStruct((rows, MLA_ROPE), F32),
        jax.ShapeDtypeStruct((MOBA_HEADS, rows, MOBA_DIM), BF16),
        jax.ShapeDtypeStruct((rows, WIDTH_B), F32),
        jax.ShapeDtypeStruct((rows, kvw), F32),
        jax.ShapeDtypeStruct((rows, kvw), F32),
        jax.ShapeDtypeStruct((rows, kvw), BF16),
        jax.ShapeDtypeStruct((rows, kvw), BF16),
        jax.ShapeDtypeStruct((nt, 1, kvw), F32),
        jax.ShapeDtypeStruct((rows, WIDTH_A), BF16),
        jax.ShapeDtypeStruct((rows, WIDTH_B), BF16),
    )
    out_specs = (
        head(MLA_QK_PAD), head(MLA_QK_PAD), row(WIDTH_A), row(MLA_KV_LORA), row(MLA_ROPE),
        head(MOBA_DIM), row(WIDTH_B), row(kvw), row(kvw), row(kvw), row(kvw),
        pl.BlockSpec((1, 1, kvw), lambda i: (i, 0, 0)), row(WIDTH_A), row(WIDTH_B),
    )
    in_specs = [
        row(D_MODEL), row(LANE), row(LANE), _const_spec((1, D_MODEL)), _const_spec((D_MODEL, EVEN_COLS)),
        _const_spec((1, MLA_Q_LORA)), _const_spec((MLA_Q_LORA, MLA_HEADS * UQ_HEAD_COLS)),
        _const_spec((1, MLA_KV_LORA)), _const_spec((1, MLA_QK_PAD)), _const_spec((1, MLA_QK_PAD)),
        _const_spec((1, MOBA_DIM)), _const_spec((1, MOBA_DIM)),
        _const_spec((MLA_KV_LORA, MLA_HEADS * (MLA_NOPE + MLA_V))),
    ]
    return pl.pallas_call(
        _proj_kernel, grid=(nt,), in_specs=in_specs, out_specs=out_specs, out_shape=out_shape,
        compiler_params=_params(("arbitrary",)),
    )(x, cs, sn, w["norm"], w["w_in"], w["qa_norm"], w["w_uq"], w["kva_norm"], w["q_gain"], w["k_gain"],
      w["moba_qn"], w["moba_kn"], w["w_ukv"])


def _gate_kernel(q_ref, km_ref, b_ref, *, nb):
    i = pl.program_id(0)
    nbp = -(-nb // SUBLANE) * SUBLANE
    tq = q_ref.shape[0]
    jrow = lax.broadcasted_iota(jnp.int32, (nbp, tq), 0)
    valid = jrow < i
    for h in range(MOBA_HEADS):
        g = h // MOBA_REP
        km = km_ref[:nbp, g * MOBA_DIM:(g + 1) * MOBA_DIM]
        qh = q_ref[:, h * MOBA_DIM:(h + 1) * MOBA_DIM]
        gate = _mm_nt(km, qh, precision=lax.Precision.HIGHEST)
        gate = jnp.where(valid, gate, -jnp.inf)
        rank = jnp.zeros((nbp, tq), F32)
        for o in range(nb):
            other = gate[o:o + 1, :]
            ahead = (other > gate) | ((other == gate) & (o < jrow))
            rank = rank + jnp.where(ahead, 1.0, 0.0)
        keep = ((rank < MOBA_TOPK) & valid) | (jrow == i)
        bias_t = jnp.where(keep, 0.0, NEG)
        bias_t = jnp.concatenate([bias_t, jnp.zeros((LANE - nbp, tq), F32)], axis=0)
        b_ref[h] = bias_t.T.astype(BF16)


def _moba_gate(qbf, kmean, nb):
    rows = qbf.shape[0]
    assert nb <= LANE and rows == nb * MOBA_BLOCK
    kvw = MOBA_KV_HEADS * MOBA_DIM
    km = jnp.zeros((LANE, kvw), F32).at[:nb].set(kmean.reshape(nb, kvw))
    return pl.pallas_call(
        functools.partial(_gate_kernel, nb=nb), grid=(nb,),
        in_specs=[pl.BlockSpec((MOBA_BLOCK, WIDTH_B), lambda i: (i, 0)), _const_spec((LANE, kvw))],
        out_specs=pl.BlockSpec((MOBA_HEADS, MOBA_BLOCK, LANE), lambda i: (0, i, 0)),
        out_shape=jax.ShapeDtypeStruct((MOBA_HEADS, rows, LANE), BF16),
        compiler_params=_params(("arbitrary",)),
    )(qbf, km)


def _tri_pairs(n):
    ii, jj = [], []
    for i in range(n):
        for j in range(i + 1):
            ii.append(i)
            jj.append(j)
    return jnp.asarray(ii, jnp.int32), jnp.asarray(jj, jnp.int32)


def _rep(x, width):
    return x if width == LANE else jnp.concatenate([x] * (width // LANE), axis=-1)


def _flash_update(scores, values, scratch):
    probs, alphas = [], []
    for s, (m_scr, l_scr, _) in zip(scores, scratch):
        m_prev = m_scr[...]
        m_new = jnp.maximum(m_prev, jnp.max(s, axis=-1, keepdims=True))
        alpha = jnp.exp2(m_prev - m_new)
        p = jnp.exp2(s - _rep(m_new, s.shape[-1]))
        l_scr[...] = alpha * l_scr[...] + jnp.sum(p, axis=-1, keepdims=True)
        m_scr[...] = m_new
        probs.append(p.astype(BF16))
        alphas.append(alpha)
    for p, alpha, v, (_, _, acc_scr) in zip(probs, alphas, values, scratch):
        acc_scr[...] = _rep(alpha, acc_scr.shape[-1]) * acc_scr[...] + _mm(p, v)


def _flash_init(scratch):
    for m_scr, l_scr, acc_scr in scratch:
        m_scr[...] = jnp.full(m_scr.shape, NEG, F32)
        l_scr[...] = jnp.zeros(l_scr.shape, F32)
        acc_scr[...] = jnp.zeros(acc_scr.shape, F32)


MLA_HEADS_PER_STEP = 2


def _mla_prompt_kernel(ii_ref, jj_ref, q_ref, k_ref, v_ref, o_ref, m_scr, l_scr, acc_scr):
    t = pl.program_id(1)
    i = ii_ref[t]
    j = jj_ref[t]
    tq, tk = q_ref.shape[1], k_ref.shape[1]
    scratch = [(m_scr.at[a], l_scr.at[a], acc_scr.at[a]) for a in range(MLA_HEADS_PER_STEP)]

    @pl.when(j == 0)
    def _():
        _flash_init(scratch)

    values = [v_ref[:, a * MLA_V:(a + 1) * MLA_V] for a in range(MLA_HEADS_PER_STEP)]

    @pl.when(j < i)
    def _():
        _flash_update([_mm_nt(q_ref[a], k_ref[a]) for a in range(MLA_HEADS_PER_STEP)], values, scratch)

    @pl.when(j == i)
    def _():
        visible = lax.broadcasted_iota(jnp.int32, (tq, tk), 1) <= lax.broadcasted_iota(jnp.int32, (tq, tk), 0)
        scores = [jnp.where(visible, _mm_nt(q_ref[a], k_ref[a]), NEG) for a in range(MLA_HEADS_PER_STEP)]
        _flash_update(scores, values, scratch)

    @pl.when(j == i)
    def _():
        for a in range(MLA_HEADS_PER_STEP):
            o_ref[:, a * MLA_V:(a + 1) * MLA_V] = acc_scr[a] / l_scr[a]


def _mla_prompt(qa, kn, va):
    rows = qa.shape[1]
    t = min(512, rows)
    assert rows % t == 0
    hp = MLA_HEADS_PER_STEP
    ii, jj = _tri_pairs(rows // t)
    grid_spec = pltpu.PrefetchScalarGridSpec(
        num_scalar_prefetch=2, grid=(MLA_HEADS // hp, ii.shape[0]),
        in_specs=[
            pl.BlockSpec((hp, t, MLA_QK_PAD), lambda h, s, ii, jj: (h, ii[s], 0)),
            pl.BlockSpec((hp, t, MLA_QK_PAD), lambda h, s, ii, jj: (h, jj[s], 0)),
            pl.BlockSpec((t, hp * MLA_V), lambda h, s, ii, jj: (jj[s], h)),
        ],
        out_specs=pl.BlockSpec((t, hp * MLA_V), lambda h, s, ii, jj: (ii[s], h)),
        scratch_shapes=[pltpu.VMEM((hp, t, LANE), F32), pltpu.VMEM((hp, t, LANE), F32),
                        pltpu.VMEM((hp, t, MLA_V), F32)],
    )
    return pl.pallas_call(
        _mla_prompt_kernel, grid_spec=grid_spec, out_shape=jax.ShapeDtypeStruct((rows, WIDTH_A), F32),
        compiler_params=_params(("arbitrary", "arbitrary")),
    )(ii, jj, qa, kn, va)


def _moba_prompt_kernel(ii_ref, jj_ref, q_ref, b_ref, k_ref, v_ref, o_ref, m_scr, l_scr, acc_scr):
    t = pl.program_id(0)
    i = ii_ref[t]
    j = jj_ref[t]
    blk = MOBA_BLOCK
    nrow = MOBA_REP * blk
    scratch = [(m_scr.at[g], l_scr.at[g], acc_scr.at[g]) for g in range(MOBA_KV_HEADS)]

    @pl.when(j == 0)
    def _():
        _flash_init(scratch)

    onehot = jnp.where(lax.broadcasted_iota(jnp.int32, (blk, LANE), 1) == j, 1.0, 0.0).astype(BF16)
    def raw_scores():
        out = []
        for g in range(MOBA_KV_HEADS):
            heads = slice(g * MOBA_REP, (g + 1) * MOBA_REP)
            q_aug = jnp.concatenate([q_ref[heads].reshape(nrow, MOBA_DIM), b_ref[heads].reshape(nrow, LANE)], axis=-1)
            k_aug = jnp.concatenate([k_ref[:, g * MOBA_DIM:(g + 1) * MOBA_DIM], onehot], axis=-1)
            out.append(_mm_nt(q_aug, k_aug))
        return out

    values = [v_ref[:, g * MOBA_DIM:(g + 1) * MOBA_DIM] for g in range(MOBA_KV_HEADS)]

    @pl.when(j < i)
    def _():
        _flash_update(raw_scores(), values, scratch)

    @pl.when(j == i)
    def _():
        qpos = lax.broadcasted_iota(jnp.int32, (MOBA_REP, blk, blk), 1).reshape(nrow, blk)
        kpos = lax.broadcasted_iota(jnp.int32, (nrow, blk), 1)
        _flash_update([jnp.where(kpos <= qpos, s, NEG) for s in raw_scores()], values, scratch)

    @pl.when(j == i)
    def _():
        for g in range(MOBA_KV_HEADS):
            o = acc_scr[g] / l_scr[g]
            for r in range(MOBA_REP):
                h = g * MOBA_REP + r
                o_ref[:, h * MOBA_DIM:(h + 1) * MOBA_DIM] = o[r * blk:(r + 1) * blk]


def _moba_prompt(qb16, bias, kb16, vb16):
    rows = qb16.shape[1]
    nb = rows // MOBA_BLOCK
    ii, jj = _tri_pairs(nb)
    nrow = MOBA_REP * MOBA_BLOCK
    kvw = MOBA_KV_HEADS * MOBA_DIM
    grid_spec = pltpu.PrefetchScalarGridSpec(
        num_scalar_prefetch=2, grid=(ii.shape[0],),
        in_specs=[
            pl.BlockSpec((MOBA_HEADS, MOBA_BLOCK, MOBA_DIM), lambda s, ii, jj: (0, ii[s], 0)),
            pl.BlockSpec((MOBA_HEADS, MOBA_BLOCK, LANE), lambda s, ii, jj: (0, ii[s], 0)),
            pl.BlockSpec((MOBA_BLOCK, kvw), lambda s, ii, jj: (jj[s], 0)),
            pl.BlockSpec((MOBA_BLOCK, kvw), lambda s, ii, jj: (jj[s], 0)),
        ],
        out_specs=pl.BlockSpec((MOBA_BLOCK, WIDTH_B), lambda s, ii, jj: (ii[s], 0)),
        scratch_shapes=[pltpu.VMEM((MOBA_KV_HEADS, nrow, LANE), F32), pltpu.VMEM((MOBA_KV_HEADS, nrow, LANE), F32),
                        pltpu.VMEM((MOBA_KV_HEADS, nrow, MOBA_DIM), F32)],
    )
    return pl.pallas_call(
        _moba_prompt_kernel, grid_spec=grid_spec, out_shape=jax.ShapeDtypeStruct((rows, WIDTH_B), F32),
        compiler_params=_params(("arbitrary",)),
    )(ii, jj, qb16, bias, kb16, vb16)


PAGES_PER_STEP = 16
COMBINE_UNROLL = 8


def _paged_copies(pt_ref, srcs, dsts, sem, b, j, slot):
    copies = []
    for p in range(PAGES_PER_STEP):
        page = pt_ref[b, j * PAGES_PER_STEP + p]
        for a, (src, dst) in enumerate(zip(srcs, dsts)):
            copies.append((pltpu.make_async_copy(src.at[page], dst(slot, p), sem.at[a, slot]), min(a, 1)))
    return copies


def _paged_pipeline(pt_ref, srcs, dsts, sem):
    b = pl.program_id(0)
    j = pl.program_id(1)
    nj = pl.num_programs(1)
    step = b * nj + j
    slot = step % 2

    @pl.when(step == 0)
    def _():
        for cp, prio in _paged_copies(pt_ref, srcs, dsts, sem, b, j, slot):
            cp.start(priority=prio)

    @pl.when(step + 1 < pl.num_programs(0) * nj)
    def _():
        last = j + 1 == nj
        nb_ = jnp.where(last, b + 1, b)
        nj_ = jnp.where(last, 0, j + 1)
        for cp, prio in _paged_copies(pt_ref, srcs, dsts, sem, nb_, nj_, 1 - slot):
            cp.start(priority=prio)

    for cp, _ in _paged_copies(pt_ref, srcs, dsts, sem, b, j, slot):
        cp.wait()
    return slot


def _mla_paged_step(slot, between, qbd_ref, qpe_ref, wukt_ref, wuv_ref, cnew_ref, knew_ref,
                    o_ref, lat_buf, kpe_buf, lhs_scr, m_scr, l_scr, ctx_scr):
    b = pl.program_id(0)
    j = pl.program_id(1)
    nj = pl.num_programs(1)
    nq = qbd_ref.shape[1]
    nhd = MLA_HEADS * MLA_NOPE
    scratch = [(m_scr, l_scr, ctx_scr)]

    @pl.when((b == 0) & (j == 0))
    def _():
        lhs_scr[:nhd, :] = wukt_ref[...]

    @pl.when(j == 0)
    def _():
        lhs_scr[nhd:, :] = _mm(qbd_ref[0], wukt_ref[...]).astype(BF16)
        _flash_init(scratch)

    def project(c):
        cb = c.astype(BF16)
        return cb, _mm_nt(lhs_scr[...], cb)

    def attend(cb, res, kpe_t, mask):
        n = cb.shape[0]
        kn = res[:nhd]
        n2 = jnp.sum((kn * kn).reshape(MLA_HEADS, MLA_NOPE, n), axis=1)
        pe2 = jnp.sum(kpe_t * kpe_t, axis=0, keepdims=True)
        spe = _mm(qpe_ref[0], kpe_t.astype(BF16))
        r = lax.rsqrt((n2 + pe2) * (1.0 / MLA_QK) + EPS)
        s = ((res[nhd:] + spe).reshape(nq // MLA_HEADS, MLA_HEADS, n) * r[None]).reshape(nq, n)
        if mask is not None:
            s = jnp.where(mask, s, NEG)
        _flash_update([s], [cb], scratch)

    cb, res = project(lat_buf[slot])
    between()
    attend(cb, res, jnp.concatenate([kpe_buf[slot, p] for p in range(PAGES_PER_STEP)], axis=-1), None)

    @pl.when(j == nj - 1)
    def _():
        nnew = cnew_ref.shape[1]
        qidx = lax.broadcasted_iota(jnp.int32, (nq // MLA_HEADS, MLA_HEADS, nnew), 0).reshape(nq, nnew)
        kidx = lax.broadcasted_iota(jnp.int32, (nq, nnew), 1)
        cb, res = project(cnew_ref[0])
        attend(cb, res, knew_ref[0], kidx <= qidx)
        ctx = (ctx_scr[...] / _rep(l_scr[...], MLA_KV_LORA)).astype(BF16)
        full = _mm(ctx, wuv_ref[...])
        hrow = lax.broadcasted_iota(jnp.int32, (nq // MLA_HEADS, MLA_HEADS, MLA_V), 1).reshape(nq, MLA_V)
        out = jnp.zeros((nq, MLA_V), F32)
        for h in range(MLA_HEADS):
            out = out + jnp.where(hrow == h, full[:, h * MLA_V:(h + 1) * MLA_V], 0.0)
        o_ref[0] = out


def _moba_paged_step(slot, q16_ref, qf_ref, knew_ref, vnew_ref, o_ref, k_buf, v_buf, m_s, l_s, g_s, o_s,
                     *, page_size, nb):
    j = pl.program_id(1)
    nj = pl.num_programs(1)
    nq = q16_ref.shape[1]
    blocks_per_step = PAGES_PER_STEP * page_size // MOBA_BLOCK

    def block(buf, u):
        base = u * MOBA_KV_HEADS * MOBA_BLOCK
        return jnp.concatenate(
            [buf[slot, pl.ds(base + g, MOBA_BLOCK, stride=MOBA_KV_HEADS), :] for g in range(MOBA_KV_HEADS)], axis=-1)

    q16 = q16_ref[0]
    qf = qf_ref[0]
    wide = (nq, LANE)
    k16 = []
    for u in range(blocks_per_step):
        kf = block(k_buf, u)
        kmean = jnp.mean(kf, axis=0, keepdims=True)
        g_s[j * blocks_per_step + u] = jnp.broadcast_to(jnp.sum(qf * kmean, axis=-1, keepdims=True), wide)
        k16.append(kf.astype(BF16))
    s_all = _mm_nt(q16, jnp.concatenate(k16, axis=0))
    probs = []
    for u in range(blocks_per_step):
        jb = j * blocks_per_step + u
        s = s_all[:, u * MOBA_BLOCK:(u + 1) * MOBA_BLOCK]
        m = jnp.max(s, axis=-1, keepdims=True)
        p = jnp.exp2(s - m)
        m_s[jb] = jnp.broadcast_to(m, wide)
        l_s[jb] = jnp.broadcast_to(jnp.sum(p, axis=-1, keepdims=True), wide)
        probs.append(p.astype(BF16))
    for u in range(blocks_per_step):
        o_s[j * blocks_per_step + u] = _mm(probs[u], block(v_buf, u).astype(BF16))

    @pl.when(j == nj - 1)
    def _():
        lane = lax.broadcasted_iota(jnp.int32, wide, 1)

        def gather(jb, carry):
            gm, mm, lm = carry
            hit = lane == jb
            return jnp.where(hit, g_s[jb], gm), jnp.where(hit, m_s[jb], mm), jnp.where(hit, l_s[jb], lm)
        gm, mm, lm = lax.fori_loop(
            0, nb, gather, (jnp.full(wide, -jnp.inf, F32), jnp.full(wide, NEG, F32), jnp.zeros(wide, F32)),
            unroll=COMBINE_UNROLL)

        def count(o, rank):
            other = g_s[o]
            ahead = (other > gm) | ((other == gm) & (o < lane))
            return rank + jnp.where(ahead, 1.0, 0.0)
        rank = lax.fori_loop(0, nb, count, jnp.zeros(wide, F32), unroll=COMBINE_UNROLL)
        keep = (rank < MOBA_TOPK) & (lane < nb)

        nnew = knew_ref.shape[1]
        qidx = lax.broadcasted_iota(jnp.int32, (nq // MOBA_HEADS, MOBA_HEADS, nnew), 0).reshape(nq, nnew)
        kidx = lax.broadcasted_iota(jnp.int32, (nq, nnew), 1)
        s_own = jnp.where(kidx <= qidx, _mm_nt(q16, knew_ref[0].astype(BF16)), NEG)
        m_own = jnp.max(s_own, axis=-1, keepdims=True)
        m_tot = jnp.maximum(jnp.max(jnp.where(keep, mm, NEG), axis=-1, keepdims=True), m_own)
        p_own = jnp.exp2(s_own - m_tot)
        wgt = jnp.exp2(jnp.where(keep, mm - m_tot, NEG))
        l_tot = jnp.sum(wgt * lm, axis=-1, keepdims=True) + jnp.sum(p_own, axis=-1, keepdims=True)
        acc0 = _mm(p_own.astype(BF16), vnew_ref[0].astype(BF16))

        def combine(jb, acc):
            w_col = jnp.sum(jnp.where(lane == jb, wgt, 0.0), axis=-1, keepdims=True)
            return acc + w_col * o_s[jb]
        acc = lax.fori_loop(0, nb, combine, acc0, unroll=COMBINE_UNROLL) / l_tot
        grow = lax.broadcasted_iota(jnp.int32, (nq // MOBA_HEADS, MOBA_HEADS, MOBA_DIM), 1).reshape(nq, MOBA_DIM)
        o_ref[0] = jnp.where(grow < MOBA_REP, acc[:, :MOBA_DIM], acc[:, MOBA_DIM:])


def _sample_attn_kernel(pt_ref, qbd_ref, qpe_ref, wukt_ref, wuv_ref, cnew_ref, cknew_ref, q16_ref, qf_ref,
                        knew_ref, vnew_ref, lat_hbm, kpe_hbm, k_hbm, v_hbm, oa_ref, ob_ref,
                        lat_buf, kpe_buf, k_buf, v_buf, sem, lhs_scr, m_scr, l_scr, ctx_scr, m_s, l_s, g_s, o_s,
                        *, page_size, nb):
    page_rows = MOBA_KV_HEADS * page_size
    window = lambda buf: (lambda s, p: buf.at[s, pl.ds(p * page_rows, page_rows)])
    slot = _paged_pipeline(
        pt_ref, (lat_hbm, kpe_hbm, k_hbm, v_hbm),
        (lambda s, p: lat_buf.at[s, pl.ds(p * page_size, page_size)], lambda s, p: kpe_buf.at[s, p],
         window(k_buf), window(v_buf)), sem)
    moba = functools.partial(_moba_paged_step, slot, q16_ref, qf_ref, knew_ref, vnew_ref, ob_ref, k_buf, v_buf,
                             m_s, l_s, g_s, o_s, page_size=page_size, nb=nb)
    _mla_paged_step(slot, lambda: None, qbd_ref, qpe_ref, wukt_ref, wuv_ref, cnew_ref, cknew_ref, oa_ref,
                    lat_buf, kpe_buf, lhs_scr, m_scr, l_scr, ctx_scr)
    moba()


def _sample_attention(page_table, qbd, qpe, wukt, wuv, cnew, cknew_t, q16, qf, knew, vnew,
                      cache_lat, cache_kpe_t, cache_k, cache_v):
    ns, n_pages = page_table.shape
    page_size = cache_lat.shape[1]
    past = n_pages * page_size
    assert n_pages % PAGES_PER_STEP == 0 and past % MOBA_BLOCK == 0
    assert (PAGES_PER_STEP * page_size) % MOBA_BLOCK == 0
    nb = past // MOBA_BLOCK
    assert nb <= LANE
    nq = qbd.shape[1]
    keys = PAGES_PER_STEP * page_size
    nhd = MLA_HEADS * MLA_NOPE
    kvw = MOBA_KV_HEADS * MOBA_DIM
    per_seq = lambda shape: pl.BlockSpec((1,) + shape, lambda b, j, pt: (b, 0, 0))
    hbm = pl.BlockSpec(memory_space=pl.ANY)
    grid_spec = pltpu.PrefetchScalarGridSpec(
        num_scalar_prefetch=1, grid=(ns, n_pages // PAGES_PER_STEP),
        in_specs=[
            per_seq((nq, nhd)), per_seq((nq, MLA_ROPE)),
            _const_spec((nhd, MLA_KV_LORA)), _const_spec((MLA_KV_LORA, MLA_HEADS * MLA_V)),
            per_seq(cnew.shape[1:]), per_seq(cknew_t.shape[1:]),
            per_seq((nq, kvw)), per_seq((nq, kvw)), per_seq(knew.shape[1:]), per_seq(vnew.shape[1:]),
            hbm, hbm, hbm, hbm,
        ],
        out_specs=(per_seq((nq, MLA_V)), per_seq((nq, MOBA_DIM))),
        scratch_shapes=[
            pltpu.VMEM((2, keys, MLA_KV_LORA), F32), pltpu.VMEM((2, PAGES_PER_STEP, MLA_ROPE, page_size), F32),
            pltpu.VMEM((2, keys * MOBA_KV_HEADS, MOBA_DIM), F32), pltpu.VMEM((2, keys * MOBA_KV_HEADS, MOBA_DIM), F32),
            pltpu.SemaphoreType.DMA((4, 2)),
            pltpu.VMEM((nhd + nq, MLA_KV_LORA), BF16),
            pltpu.VMEM((nq, LANE), F32), pltpu.VMEM((nq, LANE), F32), pltpu.VMEM((nq, MLA_KV_LORA), F32),
            pltpu.VMEM((nb, nq, LANE), F32), pltpu.VMEM((nb, nq, LANE), F32), pltpu.VMEM((nb, nq, LANE), F32),
            pltpu.VMEM((nb, nq, kvw), F32),
        ],
    )
    return pl.pallas_call(
        functools.partial(_sample_attn_kernel, page_size=page_size, nb=nb), grid_spec=grid_spec,
        out_shape=(jax.ShapeDtypeStruct((ns, nq, MLA_V), F32), jax.ShapeDtypeStruct((ns, nq, MOBA_DIM), F32)),
        compiler_params=_params(("arbitrary", "arbitrary")),
    )(page_table, qbd, qpe, wukt, wuv, cnew, cknew_t, q16, qf, knew, vnew, cache_lat, cache_kpe_t, cache_k, cache_v)


def _merge_kernel(x_ref, oa_ref, ob_ref, sga_ref, sgb_ref, w_ref, o_ref):
    mix = jnp.concatenate([(oa_ref[...] * sga_ref[...].astype(F32)).astype(BF16),
                           (ob_ref[...] * sgb_ref[...].astype(F32)).astype(BF16)], axis=-1)
    o_ref[...] = x_ref[...] + _mm(mix, w_ref[...])


def _even_merge(x, oa, ob, sga, sgb, w_out):
    rows = x.shape[0]
    tm = min(256, rows)
    assert rows % tm == 0
    row = lambda width: pl.BlockSpec((tm, width), lambda i: (i, 0))
    return pl.pallas_call(
        _merge_kernel, grid=(rows // tm,),
        in_specs=[row(D_MODEL), row(WIDTH_A), row(WIDTH_B), row(WIDTH_A), row(WIDTH_B),
                  _const_spec((WIDTH_A + WIDTH_B, D_MODEL))],
        out_specs=row(D_MODEL), out_shape=jax.ShapeDtypeStruct((rows, D_MODEL), F32),
        compiler_params=_params(("arbitrary",)),
    )(x, oa, ob, sga, sgb, w_out)


def _conv_kernel(x_ref, norm_ref, wb_ref, wc_ref, wh_ref, wg_ref, cw_ref, wo_ref, prev_ref,
                 o_ref, st_ref, xn_scr, acc_scr, carry_scr, *, step_major_seqs):
    i = pl.program_id(0)
    n = pl.program_id(1)
    tm = x_ref.shape[0]

    @pl.when(n == 0)
    def _():
        x = x_ref[...]
        xn_scr[...] = ((x * _rms(x, D_MODEL)) * norm_ref[...]).astype(BF16)
        acc_scr[...] = x

    xn = xn_scr[...]
    ch = _mm(xn, wc_ref[...]) * _mm(xn, wh_ref[...])
    cw = cw_ref[...]
    if step_major_seqs:
        ns = step_major_seqs
        full = jnp.concatenate([prev_ref[...], ch], axis=0)
        conv = cw[0:1] * full[0:tm] + cw[1:2] * full[ns:ns + tm] + cw[2:3] * full[2 * ns:2 * ns + tm]
        st_ref[...] = ch[tm - 2 * ns:]
    else:
        @pl.when(i == 0)
        def _():
            carry_scr[n] = jnp.zeros(carry_scr.shape[1:], F32)

        prev = carry_scr[n]
        rowi = lax.broadcasted_iota(jnp.int32, ch.shape, 0)
        m1 = jnp.where(rowi == 0, prev[SUBLANE - 1:SUBLANE], pltpu.roll(ch, 1, 0))
        m2 = jnp.where(rowi == 0, prev[SUBLANE - 2:SUBLANE - 1],
                       jnp.where(rowi == 1, prev[SUBLANE - 1:SUBLANE], pltpu.roll(ch, 2, 0)))
        conv = cw[0:1] * m2 + cw[1:2] * m1 + cw[2:3] * ch
        carry_scr[n] = ch[tm - SUBLANE:]
        st_ref[...] = ch[tm - SUBLANE:]
    y = (_mm(xn, wb_ref[...]) * conv * _silu(_mm(xn, wg_ref[...]))).astype(BF16)
    acc_scr[...] += _mm(y, wo_ref[...])

    @pl.when(n == pl.num_programs(1) - 1)
    def _():
        o_ref[...] = acc_scr[...]


def _conv_layer(x, prev, w, step_major_seqs):
    rows = x.shape[0]
    tm = rows if step_major_seqs else min(512, rows)
    tn = 512
    assert rows % tm == 0
    ncol = CONV_DIM // tn
    st_rows = prev.shape[0] if step_major_seqs else SUBLANE
    wcol = lambda k: pl.BlockSpec((D_MODEL, tn), lambda i, n, k=k: (0, n + k * ncol))
    return pl.pallas_call(
        functools.partial(_conv_kernel, step_major_seqs=step_major_seqs), grid=(rows // tm, ncol),
        in_specs=[
            pl.BlockSpec((tm, D_MODEL), lambda i, n: (i, 0)), _const_spec((1, D_MODEL)),
            wcol(0), wcol(1), wcol(2), wcol(3),
            pl.BlockSpec((CONV_WIDTH, tn), lambda i, n: (0, n)),
            pl.BlockSpec((tn, D_MODEL), lambda i, n: (n, 0)),
            pl.BlockSpec((prev.shape[0], tn), lambda i, n: (0, n)),
        ],
        out_specs=(pl.BlockSpec((tm, D_MODEL), lambda i, n: (i, 0)),
                   pl.BlockSpec((st_rows, tn), lambda i, n: (i, n))),
        out_shape=(jax.ShapeDtypeStruct((rows, D_MODEL), F32),
                   jax.ShapeDtypeStruct((rows // tm * st_rows, CONV_DIM), F32)),
        scratch_shapes=[pltpu.VMEM((tm, D_MODEL), BF16), pltpu.VMEM((tm, D_MODEL), F32),
                        pltpu.VMEM((ncol, SUBLANE, tn), F32)],
        compiler_params=_params(("arbitrary", "arbitrary")),
    )(x, w["norm"], w["w_in"], w["w_in"], w["w_in"], w["w_in"], w["conv_w"], w["w_out"], prev)


def _rot_cols(w):
    half = MLA_ROPE // 2
    return jnp.concatenate([-w[..., half:], w[..., :half]], axis=-1)


def _pad_cols(w, width):
    return jnp.pad(w, ((0, 0), (0, width - w.shape[-1])))


def _gain_pad(g):
    full = jnp.concatenate([g[:MLA_NOPE], g[MLA_NOPE:], g[MLA_NOPE:]])
    return jnp.pad(full, (0, MLA_QK_PAD - MLA_QK)).reshape(1, MLA_QK_PAD)


def _even_weights(norm, w_in, qa_norm, w_uq, kva_norm, w_uk, w_uv, q_norm, k_norm, moba_qn, moba_kn):
    o = np.cumsum((0, MLA_Q_LORA, MLA_KV_LORA, MLA_ROPE, WIDTH_B, MOBA_KV_HEADS * MOBA_DIM,
                   MOBA_KV_HEADS * MOBA_DIM, WIDTH_A, WIDTH_B))
    kpe_w = w_in[:, o[2]:o[3]]
    w_in2 = jnp.concatenate([
        w_in[:, o[0]:o[2]], _pad_cols(kpe_w, LANE), _pad_cols(_rot_cols(kpe_w), LANE), w_in[:, o[3]:],
    ], axis=-1).astype(BF16)
    uq = w_uq.reshape(MLA_Q_LORA, MLA_HEADS, MLA_QK)
    pe_w = uq[..., MLA_NOPE:]
    pad = jnp.zeros((MLA_Q_LORA, MLA_HEADS, LANE - MLA_ROPE), F32)
    uq2 = jnp.concatenate([uq[..., :MLA_NOPE], pe_w, pad, _rot_cols(pe_w), pad], axis=-1)
    return {
        "norm": norm.reshape(1, D_MODEL), "w_in": w_in2, "qa_norm": qa_norm.reshape(1, MLA_Q_LORA),
        "w_uq": uq2.reshape(MLA_Q_LORA, MLA_HEADS * UQ_HEAD_COLS).astype(BF16),
        "kva_norm": kva_norm.reshape(1, MLA_KV_LORA), "q_gain": _gain_pad(q_norm), "k_gain": _gain_pad(k_norm),
        "moba_qn": moba_qn.reshape(1, MOBA_DIM), "moba_kn": moba_kn.reshape(1, MOBA_DIM),
        "w_ukv": jnp.concatenate([w_uk, w_uv], axis=-1).astype(BF16),
    }


def _rope_tables(pos):
    half = MLA_ROPE // 2
    inv = ROPE_THETA ** (-jnp.arange(half, dtype=F32) / half)
    ang = pos.astype(F32)[:, None] * inv
    cos, sin = jnp.cos(ang), jnp.sin(ang)
    pad = jnp.zeros((pos.shape[0], LANE - MLA_ROPE), F32)
    return jnp.concatenate([cos, cos, pad], axis=-1), jnp.concatenate([sin, sin, pad], axis=-1)


def _block_diag_rows(x, nblk):
    eye = jnp.eye(nblk, dtype=x.dtype)
    out = x[..., :, None, :] * eye[:, :, None]
    return out.reshape(x.shape[:-1] + (nblk * x.shape[-1],))


def kernel(x_prompt, x_sample, cache_mla_latent, cache_mla_kpe, cache_moba_k, cache_moba_v, state_conv, page_table, norm_even, w_in_even, mla_qa_norm, mla_w_uq, mla_kva_norm, mla_w_uk, mla_w_uv, mla_q_norm, mla_k_norm, moba_q_norm, moba_k_norm, w_out_even, norm_odd, w_in_odd, conv_w, w_out_odd):
    assert x_prompt.shape[0] == 1 and norm_even.shape[0] == 1 and norm_odd.shape[0] == 1
    seq = x_prompt.shape[1]
    ns, steps = x_sample.shape[:2]
    n_pages = page_table.shape[1]
    page_size = cache_mla_latent.shape[2]
    past = n_pages * page_size
    assert seq % MOBA_BLOCK == 0 and steps <= SUBLANE
    kvw = MOBA_KV_HEADS * MOBA_DIM

    we = _even_weights(norm_even[0], w_in_even[0], mla_qa_norm[0], mla_w_uq[0], mla_kva_norm[0], mla_w_uk[0],
                       mla_w_uv[0], mla_q_norm[0], mla_k_norm[0], moba_q_norm[0], moba_k_norm[0])
    w_out_e = w_out_even[0].astype(BF16)
    wo = {"norm": norm_odd[0].reshape(1, D_MODEL), "w_in": w_in_odd[0].astype(BF16), "conv_w": conv_w[0],
          "w_out": w_out_odd[0].astype(BF16)}

    xp = x_prompt[0]
    cs, sn = _rope_tables(jnp.arange(seq))
    (qa, kn, va, lat_p, kpe_p, qb16, qbf, kb_p, vb_p, kb16, vb16, kmean, sga, sgb) = _even_project(xp, cs, sn, we, BF16)
    oa = _mla_prompt(qa, kn, va)
    bias = _moba_gate(qbf, kmean, seq // MOBA_BLOCK)
    ob = _moba_prompt(qb16, bias, kb16, vb16)
    y1p = _even_merge(xp, oa, ob, sga, sgb, w_out_e)
    y2p, st_p = _conv_layer(y1p, jnp.zeros((SUBLANE, CONV_DIM), F32), wo, 0)

    rows_s = steps * ns
    xs = x_sample.transpose(1, 0, 2).reshape(rows_s, D_MODEL)
    cs_s, sn_s = _rope_tables(past + jnp.repeat(jnp.arange(steps), ns))
    (qa_s, _, _, lat_s, kpe_s, qb16_s, qbf_s, kb_s, vb_s, _, _, _, sga_s, sgb_s) = _even_project(xs, cs_s, sn_s, we, F32)

    def per_seq(a):
        return a.reshape(steps, ns, a.shape[-1]).transpose(1, 0, 2)

    def pad_new(a):
        return jnp.pad(a, ((0, 0), (0, LANE - steps), (0, 0)))

    nq = steps * MLA_HEADS
    qk = qa_s.reshape(MLA_HEADS, steps, ns, MLA_QK_PAD).transpose(2, 1, 0, 3) * we["k_gain"][0]
    qbd = _block_diag_rows(qk[..., :MLA_NOPE], MLA_HEADS).reshape(ns, nq, MLA_HEADS * MLA_NOPE).astype(BF16)
    qpe = qk[..., MLA_NOPE:MLA_QK].reshape(ns, nq, MLA_ROPE).astype(BF16)

    def group_diag(a):
        g = a.reshape(ns, steps, MOBA_KV_HEADS, MOBA_REP, MOBA_DIM).transpose(0, 1, 3, 2, 4)
        return _block_diag_rows(g, MOBA_KV_HEADS).transpose(0, 1, 3, 2, 4).reshape(ns, nq, kvw)

    q16_s = group_diag(qb16_s.reshape(MOBA_HEADS, steps, ns, MOBA_DIM).transpose(2, 1, 0, 3))
    qf_s = group_diag(per_seq(qbf_s).reshape(ns, steps, MOBA_HEADS, MOBA_DIM))
    oa_s, ob_s = _sample_attention(
        page_table, qbd, qpe, mla_w_uk[0].T.astype(BF16), mla_w_uv[0].astype(BF16),
        pad_new(per_seq(lat_s)), pad_new(per_seq(kpe_s)).transpose(0, 2, 1),
        q16_s, qf_s, pad_new(per_seq(kb_s)), pad_new(per_seq(vb_s)),
        cache_mla_latent[0], cache_mla_kpe[0].transpose(0, 2, 1),
        cache_moba_k[0].reshape(-1, page_size * MOBA_KV_HEADS, MOBA_DIM),
        cache_moba_v[0].reshape(-1, page_size * MOBA_KV_HEADS, MOBA_DIM))

    def step_major(o):
        return o.reshape(ns, steps, -1).transpose(1, 0, 2).reshape(rows_s, -1)

    y1s = _even_merge(xs, step_major(oa_s), step_major(ob_s), sga_s, sgb_s, w_out_e)
    prev_s = state_conv[0].transpose(1, 0, 2).reshape((CONV_WIDTH - 1) * ns, CONV_DIM)
    y2s, st_s = _conv_layer(y1s, prev_s, wo, ns)

    return (
        y2p[None], per_seq(y2s),
        lat_p[None, None], kpe_p[None, None],
        kb_p.reshape(1, 1, seq, MOBA_KV_HEADS, MOBA_DIM), vb_p.reshape(1, 1, seq, MOBA_KV_HEADS, MOBA_DIM),
        st_p[st_p.shape[0] - (CONV_WIDTH - 1):][None, None],
        per_seq(lat_s)[None], per_seq(kpe_s)[None],
        per_seq(kb_s).reshape(1, ns, steps, MOBA_KV_HEADS, MOBA_DIM),
        per_seq(vb_s).reshape(1, ns, steps, MOBA_KV_HEADS, MOBA_DIM),
        st_s.reshape(CONV_WIDTH - 1, ns, CONV_DIM).transpose(1, 0, 2)[None],
    )
```

```python
import functools

import jax
import jax.numpy as jnp
import numpy as np
from jax import lax
from jax.experimental import pallas as pl
from jax.experimental.pallas import tpu as pltpu

F32 = jnp.float32
BF16 = jnp.bfloat16

D_MODEL = 2048
MLA_HEADS = 8
MLA_NOPE = 128
MLA_ROPE = 64
MLA_QK = MLA_NOPE + MLA_ROPE
MLA_QK_PAD = 256
MLA_V = 128
MLA_Q_LORA = 512
MLA_KV_LORA = 512
ROPE_THETA = 10000.0
MOBA_HEADS = 8
MOBA_KV_HEADS = 2
MOBA_REP = MOBA_HEADS // MOBA_KV_HEADS
MOBA_DIM = 128
MOBA_BLOCK = 256
MOBA_TOPK = 3
CONV_DIM = D_MODEL
CONV_WIDTH = 3
WIDTH_A = MLA_HEADS * MLA_V
WIDTH_B = MOBA_HEADS * MOBA_DIM
EPS = 1e-6
NEG = -1e30
LOG2E = 1.4426950408889634
LANE = 128
SUBLANE = 8
VMEM_LIMIT = 56 * 1024 * 1024

O_QLAT = 0
O_C = O_QLAT + MLA_Q_LORA
O_KPE = O_C + MLA_KV_LORA
O_QB = O_KPE + 2 * LANE
O_KB = O_QB + WIDTH_B
O_VB = O_KB + MOBA_KV_HEADS * MOBA_DIM
O_GA = O_VB + MOBA_KV_HEADS * MOBA_DIM
O_GB = O_GA + WIDTH_A
EVEN_COLS = O_GB + WIDTH_B
UQ_HEAD_COLS = 3 * LANE


def _mm(a, b):
    return jnp.dot(a, b, preferred_element_type=F32)


def _mm_nt(a, b, precision=None):
    return lax.dot_general(a, b, (((1,), (1,)), ((), ())), preferred_element_type=F32, precision=precision)


def _const_spec(shape):
    return pl.BlockSpec(shape, lambda *_: (0,) * len(shape), pipeline_mode=pl.Buffered(1))


def _params(sem):
    return pltpu.CompilerParams(dimension_semantics=sem, vmem_limit_bytes=VMEM_LIMIT)


def _rms(x, width):
    return lax.rsqrt(jnp.sum(x * x, axis=-1, keepdims=True) * (1.0 / width) + EPS)


def _silu(x):
    return x * jax.nn.sigmoid(x)


def _proj_kernel(x_ref, cs_ref, sn_ref, norm_ref, win_ref, qan_ref, wuq_ref, kvan_ref, qg_ref, kg_ref,
                 mqn_ref, mkn_ref, wukv_ref,
                 qa_ref, kn_ref, va_ref, lat_ref, kpe_ref, qb16_ref, qbf_ref, kb_ref, vb_ref, kb16_ref,
                 vb16_ref, kmean_ref, sga_ref, sgb_ref):
    x = x_ref[...]
    xn = ((x * _rms(x, D_MODEL)) * norm_ref[...]).astype(BF16)
    cs = cs_ref[...]
    sn = sn_ref[...]

    ql = _mm(xn, win_ref[:, O_QLAT:O_QLAT + MLA_Q_LORA])
    qln = ((ql * _rms(ql, MLA_Q_LORA)) * qan_ref[...]).astype(BF16)
    qh_all = _mm(qln, wuq_ref[...])
    qg = qg_ref[...]
    q_scale = MLA_QK ** -0.5 * LOG2E
    for h in range(MLA_HEADS):
        base = h * UQ_HEAD_COLS
        nope = qh_all[:, base:base + LANE]
        pe = qh_all[:, base + LANE:base + 2 * LANE] * cs + qh_all[:, base + 2 * LANE:base + 3 * LANE] * sn
        ssq = jnp.sum(nope * nope, axis=-1, keepdims=True) + jnp.sum(pe * pe, axis=-1, keepdims=True)
        rr = lax.rsqrt(ssq * (1.0 / MLA_QK) + EPS) * q_scale
        qa_ref[h] = jnp.concatenate([nope * rr * qg[:, :LANE], pe * rr * qg[:, LANE:]], axis=-1).astype(qa_ref.dtype)

    cz = _mm(xn, win_ref[:, O_C:O_C + MLA_KV_LORA])
    c = (cz * _rms(cz, MLA_KV_LORA)) * kvan_ref[...]
    lat_ref[...] = c
    kp = _mm(xn, win_ref[:, O_KPE:O_KPE + 2 * LANE])
    kpe_t = kp[:, :LANE] * cs + kp[:, LANE:] * sn
    kpe_ref[...] = kpe_t[:, :MLA_ROPE]
    kv = _mm(c.astype(BF16), wukv_ref[...])
    va_ref[...] = kv[:, MLA_HEADS * MLA_NOPE:].astype(BF16)
    pes = jnp.sum(kpe_t * kpe_t, axis=-1, keepdims=True)
    kg = kg_ref[...]
    for h in range(MLA_HEADS):
        kh = kv[:, h * MLA_NOPE:(h + 1) * MLA_NOPE]
        rr = lax.rsqrt((jnp.sum(kh * kh, axis=-1, keepdims=True) + pes) * (1.0 / MLA_QK) + EPS)
        kn_ref[h] = jnp.concatenate([kh * rr * kg[:, :LANE], kpe_t * rr * kg[:, LANE:]], axis=-1).astype(BF16)

    qb = _mm(xn, win_ref[:, O_QB:O_QB + WIDTH_B])
    m_scale = MOBA_DIM ** -0.5 * LOG2E
    for h in range(MOBA_HEADS):
        t = qb[:, h * MOBA_DIM:(h + 1) * MOBA_DIM]
        t = (t * _rms(t, MOBA_DIM)) * mqn_ref[...]
        qbf_ref[:, h * MOBA_DIM:(h + 1) * MOBA_DIM] = t
        qb16_ref[h] = (t * m_scale).astype(BF16)
    kbz = _mm(xn, win_ref[:, O_KB:O_KB + MOBA_KV_HEADS * MOBA_DIM])
    kparts = []
    for g in range(MOBA_KV_HEADS):
        t = kbz[:, g * MOBA_DIM:(g + 1) * MOBA_DIM]
        kparts.append((t * _rms(t, MOBA_DIM)) * mkn_ref[...])
    kb = jnp.concatenate(kparts, axis=-1)
    kb_ref[...] = kb
    kb16_ref[...] = kb.astype(BF16)
    kmean_ref[0] = jnp.mean(kb, axis=0, keepdims=True)
    vb = _mm(xn, win_ref[:, O_VB:O_VB + MOBA_KV_HEADS * MOBA_DIM])
    vb_ref[...] = vb
    vb16_ref[...] = vb.astype(BF16)

    sga_ref[...] = _silu(_mm(xn, win_ref[:, O_GA:O_GA + WIDTH_A])).astype(BF16)
    sgb_ref[...] = _silu(_mm(xn, win_ref[:, O_GB:O_GB + WIDTH_B])).astype(BF16)


def _even_project(x, cs, sn, w, q_dtype):
    rows = x.shape[0]
    tm = min(MOBA_BLOCK, rows)
    assert rows % tm == 0
    nt = rows // tm
    row = lambda width: pl.BlockSpec((tm, width), lambda i: (i, 0))
    head = lambda width: pl.BlockSpec((MLA_HEADS, tm, width), lambda i: (0, i, 0))
    kvw = MOBA_KV_HEADS * MOBA_DIM
    out_shape = (
        jax.ShapeDtypeStruct((MLA_HEADS, rows, MLA_QK_PAD), q_dtype),
        jax.ShapeDtypeStruct((MLA_HEADS, rows, MLA_QK_PAD), BF16),
        jax.ShapeDtypeStruct((rows, WIDTH_A), BF16),
        jax.ShapeDtypeStruct((rows, MLA_KV_LORA), F32),
        jax.ShapeDtypeStruct((rows, MLA_ROPE), F32),
        jax.ShapeDtypeStruct((MOBA_HEADS, rows, MOBA_DIM), BF16),
        jax.ShapeDtypeStruct((rows, WIDTH_B), F32),
        jax.ShapeDtypeStruct((rows, kvw), F32),
        jax.ShapeDtypeStruct((rows, kvw), F32),
        jax.ShapeDtypeStruct((rows, kvw), BF16),
        jax.ShapeDtypeStruct((rows, kvw), BF16),
        jax.ShapeDtypeStruct((nt, 1, kvw), F32),
        jax.ShapeDtypeStruct((rows, WIDTH_A), BF16),
        jax.ShapeDtypeStruct((rows, WIDTH_B), BF16),
    )
    out_specs = (
        head(MLA_QK_PAD), head(MLA_QK_PAD), row(WIDTH_A), row(MLA_KV_LORA), row(MLA_ROPE),
        head(MOBA_DIM), row(WIDTH_B), row(kvw), row(kvw), row(kvw), row(kvw),
        pl.BlockSpec((1, 1, kvw), lambda i: (i, 0, 0)), row(WIDTH_A), row(WIDTH_B),
    )
    in_specs = [
        row(D_MODEL), row(LANE), row(LANE), _const_spec((1, D_MODEL)), _const_spec((D_MODEL, EVEN_COLS)),
        _const_spec((1, MLA_Q_LORA)), _const_spec((MLA_Q_LORA, MLA_HEADS * UQ_HEAD_COLS)),
        _const_spec((1, MLA_KV_LORA)), _const_spec((1, MLA_QK_PAD)), _const_spec((1, MLA_QK_PAD)),
        _const_spec((1, MOBA_DIM)), _const_spec((1, MOBA_DIM)),
        _const_spec((MLA_KV_LORA, MLA_HEADS * (MLA_NOPE + MLA_V))),
    ]
    return pl.pallas_call(
        _proj_kernel, grid=(nt,), in_specs=in_specs, out_specs=out_specs, out_shape=out_shape,
        compiler_params=_params(("arbitrary",)),
    )(x, cs, sn, w["norm"], w["w_in"], w["qa_norm"], w["w_uq"], w["kva_norm"], w["q_gain"], w["k_gain"],
      w["moba_qn"], w["moba_kn"], w["w_ukv"])


def _gate_kernel(q_ref, km_ref, b_ref, *, nb):
    i = pl.program_id(0)
    nbp = -(-nb // SUBLANE) * SUBLANE
    tq = q_ref.shape[0]
    jrow = lax.broadcasted_iota(jnp.int32, (nbp, tq), 0)
    valid = jrow < i
    for h in range(MOBA_HEADS):
        g = h // MOBA_REP
        km = km_ref[:nbp, g * MOBA_DIM:(g + 1) * MOBA_DIM]
        qh = q_ref[:, h * MOBA_DIM:(h + 1) * MOBA_DIM]
        gate = _mm_nt(km, qh, precision=lax.Precision.HIGHEST)
        gate = jnp.where(valid, gate, -jnp.inf)
        rank = jnp.zeros((nbp, tq), F32)
        for o in range(nb):
            other = gate[o:o + 1, :]
            ahead = (other > gate) | ((other == gate) & (o < jrow))
            rank = rank + jnp.where(ahead, 1.0, 0.0)
        keep = ((rank < MOBA_TOPK) & valid) | (jrow == i)
        bias_t = jnp.where(keep, 0.0, NEG)
        bias_t = jnp.concatenate([bias_t, jnp.zeros((LANE - nbp, tq), F32)], axis=0)
        b_ref[h] = bias_t.T.astype(BF16)


def _moba_gate(qbf, kmean, nb):
    rows = qbf.shape[0]
    assert nb <= LANE and rows == nb * MOBA_BLOCK
    kvw = MOBA_KV_HEADS * MOBA_DIM
    km = jnp.zeros((LANE, kvw), F32).at[:nb].set(kmean.reshape(nb, kvw))
    return pl.pallas_call(
        functools.partial(_gate_kernel, nb=nb), grid=(nb,),
        in_specs=[pl.BlockSpec((MOBA_BLOCK, WIDTH_B), lambda i: (i, 0)), _const_spec((LANE, kvw))],
        out_specs=pl.BlockSpec((MOBA_HEADS, MOBA_BLOCK, LANE), lambda i: (0, i, 0)),
        out_shape=jax.ShapeDtypeStruct((MOBA_HEADS, rows, LANE), BF16),
        compiler_params=_params(("arbitrary",)),
    )(qbf, km)


def _tri_pairs(n):
    ii, jj = [], []
    for i in range(n):
        for j in range(i + 1):
            ii.append(i)
            jj.append(j)
    return jnp.asarray(ii, jnp.int32), jnp.asarray(jj, jnp.int32)


def _rep(x, width):
    return x if width == LANE else jnp.concatenate([x] * (width // LANE), axis=-1)


def _flash_update(scores, values, scratch):
    probs, alphas = [], []
    for s, (m_scr, l_scr, _) in zip(scores, scratch):
        m_prev = m_scr[...]
        m_new = jnp.maximum(m_prev, jnp.max(s, axis=-1, keepdims=True))
        alpha = jnp.exp2(m_prev - m_new)
        p = jnp.exp2(s - _rep(m_new, s.shape[-1]))
        l_scr[...] = alpha * l_scr[...] + jnp.sum(p, axis=-1, keepdims=True)
        m_scr[...] = m_new
        probs.append(p.astype(BF16))
        alphas.append(alpha)
    for p, alpha, v, (_, _, acc_scr) in zip(probs, alphas, values, scratch):
        acc_scr[...] = _rep(alpha, acc_scr.shape[-1]) * acc_scr[...] + _mm(p, v)


def _flash_init(scratch):
    for m_scr, l_scr, acc_scr in scratch:
        m_scr[...] = jnp.full(m_scr.shape, NEG, F32)
        l_scr[...] = jnp.zeros(l_scr.shape, F32)
        acc_scr[...] = jnp.zeros(acc_scr.shape, F32)


MLA_HEADS_PER_STEP = 4


def _mla_prompt_kernel(ii_ref, jj_ref, q_ref, k_ref, v_ref, o_ref, m_scr, l_scr, acc_scr):
    t = pl.program_id(1)
    i = ii_ref[t]
    j = jj_ref[t]
    tq, tk = q_ref.shape[1], k_ref.shape[1]
    scratch = [(m_scr.at[a], l_scr.at[a], acc_scr.at[a]) for a in range(MLA_HEADS_PER_STEP)]

    @pl.when(j == 0)
    def _():
        _flash_init(scratch)

    values = [v_ref[:, a * MLA_V:(a + 1) * MLA_V] for a in range(MLA_HEADS_PER_STEP)]

    @pl.when(j < i)
    def _():
        _flash_update([_mm_nt(q_ref[a], k_ref[a]) for a in range(MLA_HEADS_PER_STEP)], values, scratch)

    @pl.when(j == i)
    def _():
        visible = lax.broadcasted_iota(jnp.int32, (tq, tk), 1) <= lax.broadcasted_iota(jnp.int32, (tq, tk), 0)
        scores = [jnp.where(visible, _mm_nt(q_ref[a], k_ref[a]), NEG) for a in range(MLA_HEADS_PER_STEP)]
        _flash_update(scores, values, scratch)

    @pl.when(j == i)
    def _():
        for a in range(MLA_HEADS_PER_STEP):
            o_ref[:, a * MLA_V:(a + 1) * MLA_V] = acc_scr[a] / l_scr[a]


def _mla_prompt(qa, kn, va):
    rows = qa.shape[1]
    t = min(512, rows)
    assert rows % t == 0
    hp = MLA_HEADS_PER_STEP
    ii, jj = _tri_pairs(rows // t)
    grid_spec = pltpu.PrefetchScalarGridSpec(
        num_scalar_prefetch=2, grid=(MLA_HEADS // hp, ii.shape[0]),
        in_specs=[
            pl.BlockSpec((hp, t, MLA_QK_PAD), lambda h, s, ii, jj: (h, ii[s], 0)),
            pl.BlockSpec((hp, t, MLA_QK_PAD), lambda h, s, ii, jj: (h, jj[s], 0)),
            pl.BlockSpec((t, hp * MLA_V), lambda h, s, ii, jj: (jj[s], h)),
        ],
        out_specs=pl.BlockSpec((t, hp * MLA_V), lambda h, s, ii, jj: (ii[s], h)),
        scratch_shapes=[pltpu.VMEM((hp, t, LANE), F32), pltpu.VMEM((hp, t, LANE), F32),
                        pltpu.VMEM((hp, t, MLA_V), F32)],
    )
    return pl.pallas_call(
        _mla_prompt_kernel, grid_spec=grid_spec, out_shape=jax.ShapeDtypeStruct((rows, WIDTH_A), F32),
        compiler_params=_params(("arbitrary", "arbitrary")),
    )(ii, jj, qa, kn, va)


def _moba_prompt_kernel(ii_ref, jj_ref, q_ref, b_ref, k_ref, v_ref, o_ref, m_scr, l_scr, acc_scr):
    t = pl.program_id(0)
    i = ii_ref[t]
    j = jj_ref[t]
    blk = MOBA_BLOCK
    nrow = MOBA_REP * blk
    scratch = [(m_scr.at[g], l_scr.at[g], acc_scr.at[g]) for g in range(MOBA_KV_HEADS)]

    @pl.when(j == 0)
    def _():
        _flash_init(scratch)

    onehot = jnp.where(lax.broadcasted_iota(jnp.int32, (blk, LANE), 1) == j, 1.0, 0.0).astype(BF16)
    def raw_scores():
        out = []
        for g in range(MOBA_KV_HEADS):
            heads = slice(g * MOBA_REP, (g + 1) * MOBA_REP)
            q_aug = jnp.concatenate([q_ref[heads].reshape(nrow, MOBA_DIM), b_ref[heads].reshape(nrow, LANE)], axis=-1)
            k_aug = jnp.concatenate([k_ref[:, g * MOBA_DIM:(g + 1) * MOBA_DIM], onehot], axis=-1)
            out.append(_mm_nt(q_aug, k_aug))
        return out

    values = [v_ref[:, g * MOBA_DIM:(g + 1) * MOBA_DIM] for g in range(MOBA_KV_HEADS)]

    @pl.when(j < i)
    def _():
        _flash_update(raw_scores(), values, scratch)

    @pl.when(j == i)
    def _():
        qpos = lax.broadcasted_iota(jnp.int32, (MOBA_REP, blk, blk), 1).reshape(nrow, blk)
        kpos = lax.broadcasted_iota(jnp.int32, (nrow, blk), 1)
        _flash_update([jnp.where(kpos <= qpos, s, NEG) for s in raw_scores()], values, scratch)

    @pl.when(j == i)
    def _():
        for g in range(MOBA_KV_HEADS):
            o = acc_scr[g] / l_scr[g]
            for r in range(MOBA_REP):
                h = g * MOBA_REP + r
                o_ref[:, h * MOBA_DIM:(h + 1) * MOBA_DIM] = o[r * blk:(r + 1) * blk]


def _moba_prompt(qb16, bias, kb16, vb16):
    rows = qb16.shape[1]
    nb = rows // MOBA_BLOCK
    ii, jj = _tri_pairs(nb)
    nrow = MOBA_REP * MOBA_BLOCK
    kvw = MOBA_KV_HEADS * MOBA_DIM
    grid_spec = pltpu.PrefetchScalarGridSpec(
        num_scalar_prefetch=2, grid=(ii.shape[0],),
        in_specs=[
            pl.BlockSpec((MOBA_HEADS, MOBA_BLOCK, MOBA_DIM), lambda s, ii, jj: (0, ii[s], 0)),
            pl.BlockSpec((MOBA_HEADS, MOBA_BLOCK, LANE), lambda s, ii, jj: (0, ii[s], 0)),
            pl.BlockSpec((MOBA_BLOCK, kvw), lambda s, ii, jj: (jj[s], 0)),
            pl.BlockSpec((MOBA_BLOCK, kvw), lambda s, ii, jj: (jj[s], 0)),
        ],
        out_specs=pl.BlockSpec((MOBA_BLOCK, WIDTH_B), lambda s, ii, jj: (ii[s], 0)),
        scratch_shapes=[pltpu.VMEM((MOBA_KV_HEADS, nrow, LANE), F32), pltpu.VMEM((MOBA_KV_HEADS, nrow, LANE), F32),
                        pltpu.VMEM((MOBA_KV_HEADS, nrow, MOBA_DIM), F32)],
    )
    return pl.pallas_call(
        _moba_prompt_kernel, grid_spec=grid_spec, out_shape=jax.ShapeDtypeStruct((rows, WIDTH_B), F32),
        compiler_params=_params(("arbitrary",)),
    )(ii, jj, qb16, bias, kb16, vb16)


PAGES_PER_STEP = 16
COMBINE_UNROLL = 8


def _paged_copies(pt_ref, srcs, dsts, sem, b, j, slot):
    copies = []
    for p in range(PAGES_PER_STEP):
        page = pt_ref[b, j * PAGES_PER_STEP + p]
        for a, (src, dst) in enumerate(zip(srcs, dsts)):
            copies.append((pltpu.make_async_copy(src.at[page], dst(slot, p), sem.at[a, slot]), min(a, 1)))
    return copies


def _paged_pipeline(pt_ref, srcs, dsts, sem):
    b = pl.program_id(0)
    j = pl.program_id(1)
    nj = pl.num_programs(1)
    step = b * nj + j
    slot = step % 2

    @pl.when(step == 0)
    def _():
        for cp, prio in _paged_copies(pt_ref, srcs, dsts, sem, b, j, slot):
            cp.start(priority=prio)

    @pl.when(step + 1 < pl.num_programs(0) * nj)
    def _():
        last = j + 1 == nj
        nb_ = jnp.where(last, b + 1, b)
        nj_ = jnp.where(last, 0, j + 1)
        for cp, prio in _paged_copies(pt_ref, srcs, dsts, sem, nb_, nj_, 1 - slot):
            cp.start(priority=prio)

    for cp, _ in _paged_copies(pt_ref, srcs, dsts, sem, b, j, slot):
        cp.wait()
    return slot


def _mla_paged_step(slot, between, qbd_ref, qpe_ref, wukt_ref, wuv_ref, cnew_ref, knew_ref,
                    o_ref, lat_buf, kpe_buf, lhs_scr, m_scr, l_scr, ctx_scr):
    b = pl.program_id(0)
    j = pl.program_id(1)
    nj = pl.num_programs(1)
    nq = qbd_ref.shape[1]
    nhd = MLA_HEADS * MLA_NOPE
    scratch = [(m_scr, l_scr, ctx_scr)]

    @pl.when((b == 0) & (j == 0))
    def _():
        lhs_scr[:nhd, :] = wukt_ref[...]

    @pl.when(j == 0)
    def _():
        lhs_scr[nhd:, :] = _mm(qbd_ref[0], wukt_ref[...]).astype(BF16)
        _flash_init(scratch)

    def project(c):
        cb = c.astype(BF16)
        return cb, _mm_nt(lhs_scr[...], cb)

    def attend(cb, res, kpe_t, mask):
        n = cb.shape[0]
        kn = res[:nhd]
        n2 = jnp.sum((kn * kn).reshape(MLA_HEADS, MLA_NOPE, n), axis=1)
        pe2 = jnp.sum(kpe_t * kpe_t, axis=0, keepdims=True)
        spe = _mm(qpe_ref[0], kpe_t.astype(BF16))
        r = lax.rsqrt((n2 + pe2) * (1.0 / MLA_QK) + EPS)
        s = ((res[nhd:] + spe).reshape(nq // MLA_HEADS, MLA_HEADS, n) * r[None]).reshape(nq, n)
        if mask is not None:
            s = jnp.where(mask, s, NEG)
        _flash_update([s], [cb], scratch)

    cb, res = project(lat_buf[slot])
    between()
    attend(cb, res, jnp.concatenate([kpe_buf[slot, p] for p in range(PAGES_PER_STEP)], axis=-1), None)

    @pl.when(j == nj - 1)
    def _():
        nnew = cnew_ref.shape[1]
        qidx = lax.broadcasted_iota(jnp.int32, (nq // MLA_HEADS, MLA_HEADS, nnew), 0).reshape(nq, nnew)
        kidx = lax.broadcasted_iota(jnp.int32, (nq, nnew), 1)
        cb, res = project(cnew_ref[0])
        attend(cb, res, knew_ref[0], kidx <= qidx)
        ctx = (ctx_scr[...] / _rep(l_scr[...], MLA_KV_LORA)).astype(BF16)
        full = _mm(ctx, wuv_ref[...])
        hrow = lax.broadcasted_iota(jnp.int32, (nq // MLA_HEADS, MLA_HEADS, MLA_V), 1).reshape(nq, MLA_V)
        out = jnp.zeros((nq, MLA_V), F32)
        for h in range(MLA_HEADS):
            out = out + jnp.where(hrow == h, full[:, h * MLA_V:(h + 1) * MLA_V], 0.0)
        o_ref[0] = out


def _moba_paged_step(slot, q16_ref, qf_ref, knew_ref, vnew_ref, o_ref, k_buf, v_buf, m_s, l_s, g_s, o_s,
                     *, page_size, nb):
    j = pl.program_id(1)
    nj = pl.num_programs(1)
    nq = q16_ref.shape[1]
    blocks_per_step = PAGES_PER_STEP * page_size // MOBA_BLOCK

    def block(buf, u):
        base = u * MOBA_KV_HEADS * MOBA_BLOCK
        return jnp.concatenate(
            [buf[slot, pl.ds(base + g, MOBA_BLOCK, stride=MOBA_KV_HEADS), :] for g in range(MOBA_KV_HEADS)], axis=-1)

    q16 = q16_ref[0]
    qf = qf_ref[0]
    wide = (nq, LANE)
    k16 = []
    for u in range(blocks_per_step):
        kf = block(k_buf, u)
        kmean = jnp.mean(kf, axis=0, keepdims=True)
        g_s[j * blocks_per_step + u] = jnp.broadcast_to(jnp.sum(qf * kmean, axis=-1, keepdims=True), wide)
        k16.append(kf.astype(BF16))
    s_all = _mm_nt(q16, jnp.concatenate(k16, axis=0))
    probs = []
    for u in range(blocks_per_step):
        jb = j * blocks_per_step + u
        s = s_all[:, u * MOBA_BLOCK:(u + 1) * MOBA_BLOCK]
        m = jnp.max(s, axis=-1, keepdims=True)
        p = jnp.exp2(s - m)
        m_s[jb] = jnp.broadcast_to(m, wide)
        l_s[jb] = jnp.broadcast_to(jnp.sum(p, axis=-1, keepdims=True), wide)
        probs.append(p.astype(BF16))
    for u in range(blocks_per_step):
        o_s[j * blocks_per_step + u] = _mm(probs[u], block(v_buf, u).astype(BF16))

    @pl.when(j == nj - 1)
    def _():
        lane = lax.broadcasted_iota(jnp.int32, wide, 1)

        def gather(jb, carry):
            gm, mm, lm = carry
            hit = lane == jb
            return jnp.where(hit, g_s[jb], gm), jnp.where(hit, m_s[jb], mm), jnp.where(hit, l_s[jb], lm)
        gm, mm, lm = lax.fori_loop(
            0, nb, gather, (jnp.full(wide, -jnp.inf, F32), jnp.full(wide, NEG, F32), jnp.zeros(wide, F32)),
            unroll=COMBINE_UNROLL)

        def count(o, rank):
            other = g_s[o]
            ahead = (other > gm) | ((other == gm) & (o < lane))
            return rank + jnp.where(ahead, 1.0, 0.0)
        rank = lax.fori_loop(0, nb, count, jnp.zeros(wide, F32), unroll=COMBINE_UNROLL)
        keep = (rank < MOBA_TOPK) & (lane < nb)

        nnew = knew_ref.shape[1]
        qidx = lax.broadcasted_iota(jnp.int32, (nq // MOBA_HEADS, MOBA_HEADS, nnew), 0).reshape(nq, nnew)
        kidx = lax.broadcasted_iota(jnp.int32, (nq, nnew), 1)
        s_own = jnp.where(kidx <= qidx, _mm_nt(q16, knew_ref[0].astype(BF16)), NEG)
        m_own = jnp.max(s_own, axis=-1, keepdims=True)
        m_tot = jnp.maximum(jnp.max(jnp.where(keep, mm, NEG), axis=-1, keepdims=True), m_own)
        p_own = jnp.exp2(s_own - m_tot)
        wgt = jnp.exp2(jnp.where(keep, mm - m_tot, NEG))
        l_tot = jnp.sum(wgt * lm, axis=-1, keepdims=True) + jnp.sum(p_own, axis=-1, keepdims=True)
        acc0 = _mm(p_own.astype(BF16), vnew_ref[0].astype(BF16))

        def combine(jb, acc):
            w_col = jnp.sum(jnp.where(lane == jb, wgt, 0.0), axis=-1, keepdims=True)
            return acc + w_col * o_s[jb]
        acc = lax.fori_loop(0, nb, combine, acc0, unroll=COMBINE_UNROLL) / l_tot
        grow = lax.broadcasted_iota(jnp.int32, (nq // MOBA_HEADS, MOBA_HEADS, MOBA_DIM), 1).reshape(nq, MOBA_DIM)
        o_ref[0] = jnp.where(grow < MOBA_REP, acc[:, :MOBA_DIM], acc[:, MOBA_DIM:])


def _sample_attn_kernel(pt_ref, qbd_ref, qpe_ref, wukt_ref, wuv_ref, cnew_ref, cknew_ref, q16_ref, qf_ref,
                        knew_ref, vnew_ref, lat_hbm, kpe_hbm, k_hbm, v_hbm, oa_ref, ob_ref,
                        lat_buf, kpe_buf, k_buf, v_buf, sem, lhs_scr, m_scr, l_scr, ctx_scr, m_s, l_s, g_s, o_s,
                        *, page_size, nb):
    page_rows = MOBA_KV_HEADS * page_size
    window = lambda buf: (lambda s, p: buf.at[s, pl.ds(p * page_rows, page_rows)])
    slot = _paged_pipeline(
        pt_ref, (lat_hbm, kpe_hbm, k_hbm, v_hbm),
        (lambda s, p: lat_buf.at[s, pl.ds(p * page_size, page_size)], lambda s, p: kpe_buf.at[s, p],
         window(k_buf), window(v_buf)), sem)
    moba = functools.partial(_moba_paged_step, slot, q16_ref, qf_ref, knew_ref, vnew_ref, ob_ref, k_buf, v_buf,
                             m_s, l_s, g_s, o_s, page_size=page_size, nb=nb)
    _mla_paged_step(slot, lambda: None, qbd_ref, qpe_ref, wukt_ref, wuv_ref, cnew_ref, cknew_ref, oa_ref,
                    lat_buf, kpe_buf, lhs_scr, m_scr, l_scr, ctx_scr)
    moba()


def _sample_attention(page_table, qbd, qpe, wukt, wuv, cnew, cknew_t, q16, qf, knew, vnew,
                      cache_lat, cache_kpe_t, cache_k, cache_v):
    ns, n_pages = page_table.shape
    page_size = cache_lat.shape[1]
    past = n_pages * page_size
    assert n_pages % PAGES_PER_STEP == 0 and past % MOBA_BLOCK == 0
    assert (PAGES_PER_STEP * page_size) % MOBA_BLOCK == 0
    nb = past // MOBA_BLOCK
    assert nb <= LANE
    nq = qbd.shape[1]
    keys = PAGES_PER_STEP * page_size
    nhd = MLA_HEADS * MLA_NOPE
    kvw = MOBA_KV_HEADS * MOBA_DIM
    per_seq = lambda shape: pl.BlockSpec((1,) + shape, lambda b, j, pt: (b, 0, 0))
    hbm = pl.BlockSpec(memory_space=pl.ANY)
    grid_spec = pltpu.PrefetchScalarGridSpec(
        num_scalar_prefetch=1, grid=(ns, n_pages // PAGES_PER_STEP),
        in_specs=[
            per_seq((nq, nhd)), per_seq((nq, MLA_ROPE)),
            _const_spec((nhd, MLA_KV_LORA)), _const_spec((MLA_KV_LORA, MLA_HEADS * MLA_V)),
            per_seq(cnew.shape[1:]), per_seq(cknew_t.shape[1:]),
            per_seq((nq, kvw)), per_seq((nq, kvw)), per_seq(knew.shape[1:]), per_seq(vnew.shape[1:]),
            hbm, hbm, hbm, hbm,
        ],
        out_specs=(per_seq((nq, MLA_V)), per_seq((nq, MOBA_DIM))),
        scratch_shapes=[
            pltpu.VMEM((2, keys, MLA_KV_LORA), F32), pltpu.VMEM((2, PAGES_PER_STEP, MLA_ROPE, page_size), F32),
            pltpu.VMEM((2, keys * MOBA_KV_HEADS, MOBA_DIM), F32), pltpu.VMEM((2, keys * MOBA_KV_HEADS, MOBA_DIM), F32),
            pltpu.SemaphoreType.DMA((4, 2)),
            pltpu.VMEM((nhd + nq, MLA_KV_LORA), BF16),
            pltpu.VMEM((nq, LANE), F32), pltpu.VMEM((nq, LANE), F32), pltpu.VMEM((nq, MLA_KV_LORA), F32),
            pltpu.VMEM((nb, nq, LANE), F32), pltpu.VMEM((nb, nq, LANE), F32), pltpu.VMEM((nb, nq, LANE), F32),
            pltpu.VMEM((nb, nq, kvw), F32),
        ],
    )
    return pl.pallas_call(
        functools.partial(_sample_attn_kernel, page_size=page_size, nb=nb), grid_spec=grid_spec,
        out_shape=(jax.ShapeDtypeStruct((ns, nq, MLA_V), F32), jax.ShapeDtypeStruct((ns, nq, MOBA_DIM), F32)),
        compiler_params=_params(("arbitrary", "arbitrary")),
    )(page_table, qbd, qpe, wukt, wuv, cnew, cknew_t, q16, qf, knew, vnew, cache_lat, cache_kpe_t, cache_k, cache_v)


def _merge_kernel(x_ref, oa_ref, ob_ref, sga_ref, sgb_ref, w_ref, o_ref):
    mix = jnp.concatenate([(oa_ref[...] * sga_ref[...].astype(F32)).astype(BF16),
                           (ob_ref[...] * sgb_ref[...].astype(F32)).astype(BF16)], axis=-1)
    o_ref[...] = x_ref[...] + _mm(mix, w_ref[...])


def _even_merge(x, oa, ob, sga, sgb, w_out):
    rows = x.shape[0]
    tm = min(256, rows)
    assert rows % tm == 0
    row = lambda width: pl.BlockSpec((tm, width), lambda i: (i, 0))
    return pl.pallas_call(
        _merge_kernel, grid=(rows // tm,),
        in_specs=[row(D_MODEL), row(WIDTH_A), row(WIDTH_B), row(WIDTH_A), row(WIDTH_B),
                  _const_spec((WIDTH_A + WIDTH_B, D_MODEL))],
        out_specs=row(D_MODEL), out_shape=jax.ShapeDtypeStruct((rows, D_MODEL), F32),
        compiler_params=_params(("arbitrary",)),
    )(x, oa, ob, sga, sgb, w_out)


def _conv_kernel(x_ref, norm_ref, wb_ref, wc_ref, wh_ref, wg_ref, cw_ref, wo_ref, prev_ref,
                 o_ref, st_ref, xn_scr, acc_scr, carry_scr, *, step_major_seqs):
    i = pl.program_id(0)
    n = pl.program_id(1)
    tm = x_ref.shape[0]

    @pl.when(n == 0)
    def _():
        x = x_ref[...]
        xn_scr[...] = ((x * _rms(x, D_MODEL)) * norm_ref[...]).astype(BF16)
        acc_scr[...] = x

    xn = xn_scr[...]
    ch = _mm(xn, wc_ref[...]) * _mm(xn, wh_ref[...])
    cw = cw_ref[...]
    if step_major_seqs:
        ns = step_major_seqs
        full = jnp.concatenate([prev_ref[...], ch], axis=0)
        conv = cw[0:1] * full[0:tm] + cw[1:2] * full[ns:ns + tm] + cw[2:3] * full[2 * ns:2 * ns + tm]
        st_ref[...] = ch[tm - 2 * ns:]
    else:
        @pl.when(i == 0)
        def _():
            carry_scr[n] = jnp.zeros(carry_scr.shape[1:], F32)

        prev = carry_scr[n]
        rowi = lax.broadcasted_iota(jnp.int32, ch.shape, 0)
        m1 = jnp.where(rowi == 0, prev[SUBLANE - 1:SUBLANE], pltpu.roll(ch, 1, 0))
        m2 = jnp.where(rowi == 0, prev[SUBLANE - 2:SUBLANE - 1],
                       jnp.where(rowi == 1, prev[SUBLANE - 1:SUBLANE], pltpu.roll(ch, 2, 0)))
        conv = cw[0:1] * m2 + cw[1:2] * m1 + cw[2:3] * ch
        carry_scr[n] = ch[tm - SUBLANE:]
        st_ref[...] = ch[tm - SUBLANE:]
    y = (_mm(xn, wb_ref[...]) * conv * _silu(_mm(xn, wg_ref[...]))).astype(BF16)
    acc_scr[...] += _mm(y, wo_ref[...])

    @pl.when(n == pl.num_programs(1) - 1)
    def _():
        o_ref[...] = acc_scr[...]


def _conv_layer(x, prev, w, step_major_seqs):
    rows = x.shape[0]
    tm = rows if step_major_seqs else min(512, rows)
    tn = 512
    assert rows % tm == 0
    ncol = CONV_DIM // tn
    st_rows = prev.shape[0] if step_major_seqs else SUBLANE
    wcol = lambda k: pl.BlockSpec((D_MODEL, tn), lambda i, n, k=k: (0, n + k * ncol))
    return pl.pallas_call(
        functools.partial(_conv_kernel, step_major_seqs=step_major_seqs), grid=(rows // tm, ncol),
        in_specs=[
            pl.BlockSpec((tm, D_MODEL), lambda i, n: (i, 0)), _const_spec((1, D_MODEL)),
            wcol(0), wcol(1), wcol(2), wcol(3),
            pl.BlockSpec((CONV_WIDTH, tn), lambda i, n: (0, n)),
            pl.BlockSpec((tn, D_MODEL), lambda i, n: (n, 0)),
            pl.BlockSpec((prev.shape[0], tn), lambda i, n: (0, n)),
        ],
        out_specs=(pl.BlockSpec((tm, D_MODEL), lambda i, n: (i, 0)),
                   pl.BlockSpec((st_rows, tn), lambda i, n: (i, n))),
        out_shape=(jax.ShapeDtypeStruct((rows, D_MODEL), F32),
                   jax.ShapeDtypeStruct((rows // tm * st_rows, CONV_DIM), F32)),
        scratch_shapes=[pltpu.VMEM((tm, D_MODEL), BF16), pltpu.VMEM((tm, D_MODEL), F32),
                        pltpu.VMEM((ncol, SUBLANE, tn), F32)],
        compiler_params=_params(("arbitrary", "arbitrary")),
    )(x, w["norm"], w["w_in"], w["w_in"], w["w_in"], w["w_in"], w["conv_w"], w["w_out"], prev)


def _rot_cols(w):
    half = MLA_ROPE // 2
    return jnp.concatenate([-w[..., half:], w[..., :half]], axis=-1)


def _pad_cols(w, width):
    return jnp.pad(w, ((0, 0), (0, width - w.shape[-1])))


def _gain_pad(g):
    full = jnp.concatenate([g[:MLA_NOPE], g[MLA_NOPE:], g[MLA_NOPE:]])
    return jnp.pad(full, (0, MLA_QK_PAD - MLA_QK)).reshape(1, MLA_QK_PAD)


def _even_weights(norm, w_in, qa_norm, w_uq, kva_norm, w_uk, w_uv, q_norm, k_norm, moba_qn, moba_kn):
    o = np.cumsum((0, MLA_Q_LORA, MLA_KV_LORA, MLA_ROPE, WIDTH_B, MOBA_KV_HEADS * MOBA_DIM,
                   MOBA_KV_HEADS * MOBA_DIM, WIDTH_A, WIDTH_B))
    kpe_w = w_in[:, o[2]:o[3]]
    w_in2 = jnp.concatenate([
        w_in[:, o[0]:o[2]], _pad_cols(kpe_w, LANE), _pad_cols(_rot_cols(kpe_w), LANE), w_in[:, o[3]:],
    ], axis=-1).astype(BF16)
    uq = w_uq.reshape(MLA_Q_LORA, MLA_HEADS, MLA_QK)
    pe_w = uq[..., MLA_NOPE:]
    pad = jnp.zeros((MLA_Q_LORA, MLA_HEADS, LANE - MLA_ROPE), F32)
    uq2 = jnp.concatenate([uq[..., :MLA_NOPE], pe_w, pad, _rot_cols(pe_w), pad], axis=-1)
    return {
        "norm": norm.reshape(1, D_MODEL), "w_in": w_in2, "qa_norm": qa_norm.reshape(1, MLA_Q_LORA),
        "w_uq": uq2.reshape(MLA_Q_LORA, MLA_HEADS * UQ_HEAD_COLS).astype(BF16),
        "kva_norm": kva_norm.reshape(1, MLA_KV_LORA), "q_gain": _gain_pad(q_norm), "k_gain": _gain_pad(k_norm),
        "moba_qn": moba_qn.reshape(1, MOBA_DIM), "moba_kn": moba_kn.reshape(1, MOBA_DIM),
        "w_ukv": jnp.concatenate([w_uk, w_uv], axis=-1).astype(BF16),
    }


def _rope_tables(pos):
    half = MLA_ROPE // 2
    inv = ROPE_THETA ** (-jnp.arange(half, dtype=F32) / half)
    ang = pos.astype(F32)[:, None] * inv
    cos, sin = jnp.cos(ang), jnp.sin(ang)
    pad = jnp.zeros((pos.shape[0], LANE - MLA_ROPE), F32)
    return jnp.concatenate([cos, cos, pad], axis=-1), jnp.concatenate([sin, sin, pad], axis=-1)


def _block_diag_rows(x, nblk):
    eye = jnp.eye(nblk, dtype=x.dtype)
    out = x[..., :, None, :] * eye[:, :, None]
    return out.reshape(x.shape[:-1] + (nblk * x.shape[-1],))


def kernel(x_prompt, x_sample, cache_mla_latent, cache_mla_kpe, cache_moba_k, cache_moba_v, state_conv, page_table, norm_even, w_in_even, mla_qa_norm, mla_w_uq, mla_kva_norm, mla_w_uk, mla_w_uv, mla_q_norm, mla_k_norm, moba_q_norm, moba_k_norm, w_out_even, norm_odd, w_in_odd, conv_w, w_out_odd):
    assert x_prompt.shape[0] == 1 and norm_even.shape[0] == 1 and norm_odd.shape[0] == 1
    seq = x_prompt.shape[1]
    ns, steps = x_sample.shape[:2]
    n_pages = page_table.shape[1]
    page_size = cache_mla_latent.shape[2]
    past = n_pages * page_size
    assert seq % MOBA_BLOCK == 0 and steps <= SUBLANE
    kvw = MOBA_KV_HEADS * MOBA_DIM

    we = _even_weights(norm_even[0], w_in_even[0], mla_qa_norm[0], mla_w_uq[0], mla_kva_norm[0], mla_w_uk[0],
                       mla_w_uv[0], mla_q_norm[0], mla_k_norm[0], moba_q_norm[0], moba_k_norm[0])
    w_out_e = w_out_even[0].astype(BF16)
    wo = {"norm": norm_odd[0].reshape(1, D_MODEL), "w_in": w_in_odd[0].astype(BF16), "conv_w": conv_w[0],
          "w_out": w_out_odd[0].astype(BF16)}

    xp = x_prompt[0]
    cs, sn = _rope_tables(jnp.arange(seq))
    (qa, kn, va, lat_p, kpe_p, qb16, qbf, kb_p, vb_p, kb16, vb16, kmean, sga, sgb) = _even_project(xp, cs, sn, we, BF16)
    oa = _mla_prompt(qa, kn, va)
    bias = _moba_gate(qbf, kmean, seq // MOBA_BLOCK)
    ob = _moba_prompt(qb16, bias, kb16, vb16)
    y1p = _even_merge(xp, oa, ob, sga, sgb, w_out_e)
    y2p, st_p = _conv_layer(y1p, jnp.zeros((SUBLANE, CONV_DIM), F32), wo, 0)

    rows_s = steps * ns
    xs = x_sample.transpose(1, 0, 2).reshape(rows_s, D_MODEL)
    cs_s, sn_s = _rope_tables(past + jnp.repeat(jnp.arange(steps), ns))
    (qa_s, _, _, lat_s, kpe_s, qb16_s, qbf_s, kb_s, vb_s, _, _, _, sga_s, sgb_s) = _even_project(xs, cs_s, sn_s, we, F32)

    def per_seq(a):
        return a.reshape(steps, ns, a.shape[-1]).transpose(1, 0, 2)

    def pad_new(a):
        return jnp.pad(a, ((0, 0), (0, LANE - steps), (0, 0)))

    nq = steps * MLA_HEADS
    qk = qa_s.reshape(MLA_HEADS, steps, ns, MLA_QK_PAD).transpose(2, 1, 0, 3) * we["k_gain"][0]
    qbd = _block_diag_rows(qk[..., :MLA_NOPE], MLA_HEADS).reshape(ns, nq, MLA_HEADS * MLA_NOPE).astype(BF16)
    qpe = qk[..., MLA_NOPE:MLA_QK].reshape(ns, nq, MLA_ROPE).astype(BF16)

    def group_diag(a):
        g = a.reshape(ns, steps, MOBA_KV_HEADS, MOBA_REP, MOBA_DIM).transpose(0, 1, 3, 2, 4)
        return _block_diag_rows(g, MOBA_KV_HEADS).transpose(0, 1, 3, 2, 4).reshape(ns, nq, kvw)

    q16_s = group_diag(qb16_s.reshape(MOBA_HEADS, steps, ns, MOBA_DIM).transpose(2, 1, 0, 3))
    qf_s = group_diag(per_seq(qbf_s).reshape(ns, steps, MOBA_HEADS, MOBA_DIM))
    oa_s, ob_s = _sample_attention(
        page_table, qbd, qpe, mla_w_uk[0].T.astype(BF16), mla_w_uv[0].astype(BF16),
        pad_new(per_seq(lat_s)), pad_new(per_seq(kpe_s)).transpose(0, 2, 1),
        q16_s, qf_s, pad_new(per_seq(kb_s)), pad_new(per_seq(vb_s)),
        cache_mla_latent[0], cache_mla_kpe[0].transpose(0, 2, 1),
        cache_moba_k[0].reshape(-1, page_size * MOBA_KV_HEADS, MOBA_DIM),
        cache_moba_v[0].reshape(-1, page_size * MOBA_KV_HEADS, MOBA_DIM))

    def step_major(o):
        return o.reshape(ns, steps, -1).transpose(1, 0, 2).reshape(rows_s, -1)

    y1s = _even_merge(xs, step_major(oa_s), step_major(ob_s), sga_s, sgb_s, w_out_e)
    prev_s = state_conv[0].transpose(1, 0, 2).reshape((CONV_WIDTH - 1) * ns, CONV_DIM)
    y2s, st_s = _conv_layer(y1s, prev_s, wo, ns)

    return (
        y2p[None], per_seq(y2s),
        lat_p[None, None], kpe_p[None, None],
        kb_p.reshape(1, 1, seq, MOBA_KV_HEADS, MOBA_DIM), vb_p.reshape(1, 1, seq, MOBA_KV_HEADS, MOBA_DIM),
        st_p[st_p.shape[0] - (CONV_WIDTH - 1):][None, None],
        per_seq(lat_s)[None], per_seq(kpe_s)[None],
        per_seq(kb_s).reshape(1, ns, steps, MOBA_KV_HEADS, MOBA_DIM),
        per_seq(vb_s).reshape(1, ns, steps, MOBA_KV_HEADS, MOBA_DIM),
        st_s.reshape(CONV_WIDTH - 1, ns, CONV_DIM).transpose(1, 0, 2)[None],
    )
```

```python
import functools

import jax
import jax.numpy as jnp
import numpy as np
from jax import lax
from jax.experimental import pallas as pl
from jax.experimental.pallas import tpu as pltpu

F32 = jnp.float32
BF16 = jnp.bfloat16

D_MODEL = 2048
MLA_HEADS = 8
MLA_NOPE = 128
MLA_ROPE = 64
MLA_QK = MLA_NOPE + MLA_ROPE
MLA_QK_PAD = 256
MLA_V = 128
MLA_Q_LORA = 512
MLA_KV_LORA = 512
ROPE_THETA = 10000.0
MOBA_HEADS = 8
MOBA_KV_HEADS = 2
MOBA_REP = MOBA_HEADS // MOBA_KV_HEADS
MOBA_DIM = 128
MOBA_BLOCK = 256
MOBA_TOPK = 3
CONV_DIM = D_MODEL
CONV_WIDTH = 3
WIDTH_A = MLA_HEADS * MLA_V
WIDTH_B = MOBA_HEADS * MOBA_DIM
EPS = 1e-6
NEG = -1e30
LOG2E = 1.4426950408889634
LANE = 128
SUBLANE = 8
VMEM_LIMIT = 56 * 1024 * 1024

O_QLAT = 0
O_C = O_QLAT + MLA_Q_LORA
O_KPE = O_C + MLA_KV_LORA
O_QB = O_KPE + 2 * LANE
O_KB = O_QB + WIDTH_B
O_VB = O_KB + MOBA_KV_HEADS * MOBA_DIM
O_GA = O_VB + MOBA_KV_HEADS * MOBA_DIM
O_GB = O_GA + WIDTH_A
EVEN_COLS = O_GB + WIDTH_B
UQ_HEAD_COLS = 3 * LANE


def _mm(a, b):
    return jnp.dot(a, b, preferred_element_type=F32)


def _mm_nt(a, b, precision=None):
    return lax.dot_general(a, b, (((1,), (1,)), ((), ())), preferred_element_type=F32, precision=precision)


def _const_spec(shape):
    return pl.BlockSpec(shape, lambda *_: (0,) * len(shape), pipeline_mode=pl.Buffered(1))


def _params(sem):
    return pltpu.CompilerParams(dimension_semantics=sem, vmem_limit_bytes=VMEM_LIMIT)


def _rms(x, width):
    return lax.rsqrt(jnp.sum(x * x, axis=-1, keepdims=True) * (1.0 / width) + EPS)


def _silu(x):
    return x * jax.nn.sigmoid(x)


def _proj_kernel(x_ref, cs_ref, sn_ref, norm_ref, win_ref, qan_ref, wuq_ref, kvan_ref, qg_ref, kg_ref,
                 mqn_ref, mkn_ref, wukv_ref,
                 qa_ref, kn_ref, va_ref, lat_ref, kpe_ref, qb16_ref, qbf_ref, kb_ref, vb_ref, kb16_ref,
                 vb16_ref, kmean_ref, sga_ref, sgb_ref):
    x = x_ref[...]
    xn = ((x * _rms(x, D_MODEL)) * norm_ref[...]).astype(BF16)
    cs = cs_ref[...]
    sn = sn_ref[...]

    ql = _mm(xn, win_ref[:, O_QLAT:O_QLAT + MLA_Q_LORA])
    qln = ((ql * _rms(ql, MLA_Q_LORA)) * qan_ref[...]).astype(BF16)
    qh_all = _mm(qln, wuq_ref[...])
    qg = qg_ref[...]
    q_scale = MLA_QK ** -0.5 * LOG2E
    for h in range(MLA_HEADS):
        base = h * UQ_HEAD_COLS
        nope = qh_all[:, base:base + LANE]
        pe = qh_all[:, base + LANE:base + 2 * LANE] * cs + qh_all[:, base + 2 * LANE:base + 3 * LANE] * sn
        ssq = jnp.sum(nope * nope, axis=-1, keepdims=True) + jnp.sum(pe * pe, axis=-1, keepdims=True)
        rr = lax.rsqrt(ssq * (1.0 / MLA_QK) + EPS) * q_scale
        qa_ref[h] = jnp.concatenate([nope * rr * qg[:, :LANE], pe * rr * qg[:, LANE:]], axis=-1).astype(qa_ref.dtype)

    cz = _mm(xn, win_ref[:, O_C:O_C + MLA_KV_LORA])
    c = (cz * _rms(cz, MLA_KV_LORA)) * kvan_ref[...]
    lat_ref[...] = c
    kp = _mm(xn, win_ref[:, O_KPE:O_KPE + 2 * LANE])
    kpe_t = kp[:, :LANE] * cs + kp[:, LANE:] * sn
    kpe_ref[...] = kpe_t[:, :MLA_ROPE]
    kv = _mm(c.astype(BF16), wukv_ref[...])
    va_ref[...] = kv[:, MLA_HEADS * MLA_NOPE:].astype(BF16)
    pes = jnp.sum(kpe_t * kpe_t, axis=-1, keepdims=True)
    kg = kg_ref[...]
    for h in range(MLA_HEADS):
        kh = kv[:, h * MLA_NOPE:(h + 1) * MLA_NOPE]
        rr = lax.rsqrt((jnp.sum(kh * kh, axis=-1, keepdims=True) + pes) * (1.0 / MLA_QK) + EPS)
        kn_ref[h] = jnp.concatenate([kh * rr * kg[:, :LANE], kpe_t * rr * kg[:, LANE:]], axis=-1).astype(BF16)

    qb = _mm(xn, win_ref[:, O_QB:O_QB + WIDTH_B])
    m_scale = MOBA_DIM ** -0.5 * LOG2E
    for h in range(MOBA_HEADS):
        t = qb[:, h * MOBA_DIM:(h + 1) * MOBA_DIM]
        t = (t * _rms(t, MOBA_DIM)) * mqn_ref[...]
        qbf_ref[:, h * MOBA_DIM:(h + 1) * MOBA_DIM] = t
        qb16_ref[h] = (t * m_scale).astype(BF16)
    kbz = _mm(xn, win_ref[:, O_KB:O_KB + MOBA_KV_HEADS * MOBA_DIM])
    kparts = []
    for g in range(MOBA_KV_HEADS):
        t = kbz[:, g * MOBA_DIM:(g + 1) * MOBA_DIM]
        kparts.append((t * _rms(t, MOBA_DIM)) * mkn_ref[...])
    kb = jnp.concatenate(kparts, axis=-1)
    kb_ref[...] = kb
    kb16_ref[...] = kb.astype(BF16)
    kmean_ref[0] = jnp.mean(kb, axis=0, keepdims=True)
    vb = _mm(xn, win_ref[:, O_VB:O_VB + MOBA_KV_HEADS * MOBA_DIM])
    vb_ref[...] = vb
    vb16_ref[...] = vb.astype(BF16)

    sga_ref[...] = _silu(_mm(xn, win_ref[:, O_GA:O_GA + WIDTH_A])).astype(BF16)
    sgb_ref[...] = _silu(_mm(xn, win_ref[:, O_GB:O_GB + WIDTH_B])).astype(BF16)


def _even_project(x, cs, sn, w, q_dtype):
    rows = x.shape[0]
    tm = min(MOBA_BLOCK, rows)
    assert rows % tm == 0
    nt = rows // tm
    row = lambda width: pl.BlockSpec((tm, width), lambda i: (i, 0))
    head = lambda width: pl.BlockSpec((MLA_HEADS, tm, width), lambda i: (0, i, 0))
    kvw = MOBA_KV_HEADS * MOBA_DIM
    out_shape = (
        jax.ShapeDtypeStruct((MLA_HEADS, rows, MLA_QK_PAD), q_dtype),
        jax.ShapeDtypeStruct((MLA_HEADS, rows, MLA_QK_PAD), BF16),
        jax.ShapeDtypeStruct((rows, WIDTH_A), BF16),
        jax.ShapeDtypeStruct((rows, MLA_KV_LORA), F32),
        jax.ShapeDtypeStruct((rows, MLA_ROPE), F32),
        jax.ShapeDtypeStruct((MOBA_HEADS, rows, MOBA_DIM), BF16),
        jax.ShapeDtypeStruct((rows, WIDTH_B), F32),
        jax.ShapeDtypeStruct((rows, kvw), F32),
        jax.ShapeDtypeStruct((rows, kvw), F32),
        jax.ShapeDtypeStruct((rows, kvw), BF16),
        jax.ShapeDtypeStruct((rows, kvw), BF16),
        jax.ShapeDtypeStruct((nt, 1, kvw), F32),
        jax.ShapeDtypeStruct((rows, WIDTH_A), BF16),
        jax.ShapeDtypeStruct((rows, WIDTH_B), BF16),
    )
    out_specs = (
        head(MLA_QK_PAD), head(MLA_QK_PAD), row(WIDTH_A), row(MLA_KV_LORA), row(MLA_ROPE),
        head(MOBA_DIM), row(WIDTH_B), row(kvw), row(kvw), row(kvw), row(kvw),
        pl.BlockSpec((1, 1, kvw), lambda i: (i, 0, 0)), row(WIDTH_A), row(WIDTH_B),
    )
    in_specs = [
        row(D_MODEL), row(LANE), row(LANE), _const_spec((1, D_MODEL)), _const_spec((D_MODEL, EVEN_COLS)),
        _const_spec((1, MLA_Q_LORA)), _const_spec((MLA_Q_LORA, MLA_HEADS * UQ_HEAD_COLS)),
        _const_spec((1, MLA_KV_LORA)), _const_spec((1, MLA_QK_PAD)), _const_spec((1, MLA_QK_PAD)),
        _const_spec((1, MOBA_DIM)), _const_spec((1, MOBA_DIM)),
        _const_spec((MLA_KV_LORA, MLA_HEADS * (MLA_NOPE + MLA_V))),
    ]
    return pl.pallas_call(
        _proj_kernel, grid=(nt,), in_specs=in_specs, out_specs=out_specs, out_shape=out_shape,
        compiler_params=_params(("arbitrary",)),
    )(x, cs, sn, w["norm"], w["w_in"], w["qa_norm"], w["w_uq"], w["kva_norm"], w["q_gain"], w["k_gain"],
      w["moba_qn"], w["moba_kn"], w["w_ukv"])


def _gate_kernel(q_ref, km_ref, b_ref, *, nb):
    i = pl.program_id(0)
    nbp = -(-nb // SUBLANE) * SUBLANE
    tq = q_ref.shape[0]
    jrow = lax.broadcasted_iota(jnp.int32, (nbp, tq), 0)
    valid = jrow < i
    for h in range(MOBA_HEADS):
        g = h // MOBA_REP
        km = km_ref[:nbp, g * MOBA_DIM:(g + 1) * MOBA_DIM]
        qh = q_ref[:, h * MOBA_DIM:(h + 1) * MOBA_DIM]
        gate = _mm_nt(km, qh, precision=lax.Precision.HIGHEST)
        gate = jnp.where(valid, gate, -jnp.inf)
        rank = jnp.zeros((nbp, tq), F32)
        for o in range(nb):
            other = gate[o:o + 1, :]
            ahead = (other > gate) | ((other == gate) & (o < jrow))
            rank = rank + jnp.where(ahead, 1.0, 0.0)
        keep = ((rank < MOBA_TOPK) & valid) | (jrow == i)
        bias_t = jnp.where(keep, 0.0, NEG)
        bias_t = jnp.concatenate([bias_t, jnp.zeros((LANE - nbp, tq), F32)], axis=0)
        b_ref[h] = bias_t.T.astype(BF16)


def _moba_gate(qbf, kmean, nb):
    rows = qbf.shape[0]
    assert nb <= LANE and rows == nb * MOBA_BLOCK
    kvw = MOBA_KV_HEADS * MOBA_DIM
    km = jnp.zeros((LANE, kvw), F32).at[:nb].set(kmean.reshape(nb, kvw))
    return pl.pallas_call(
        functools.partial(_gate_kernel, nb=nb), grid=(nb,),
        in_specs=[pl.BlockSpec((MOBA_BLOCK, WIDTH_B), lambda i: (i, 0)), _const_spec((LANE, kvw))],
        out_specs=pl.BlockSpec((MOBA_HEADS, MOBA_BLOCK, LANE), lambda i: (0, i, 0)),
        out_shape=jax.ShapeDtypeStruct((MOBA_HEADS, rows, LANE), BF16),
        compiler_params=_params(("arbitrary",)),
    )(qbf, km)


def _tri_pairs(n):
    ii, jj = [], []
    for i in range(n):
        for j in range(i + 1):
            ii.append(i)
            jj.append(j)
    return jnp.asarray(ii, jnp.int32), jnp.asarray(jj, jnp.int32)


def _rep(x, width):
    return x if width == LANE else jnp.concatenate([x] * (width // LANE), axis=-1)


def _flash_update(scores, values, scratch):
    probs, alphas = [], []
    for s, (m_scr, l_scr, _) in zip(scores, scratch):
        m_prev = m_scr[...]
        m_new = jnp.maximum(m_prev, jnp.max(s, axis=-1, keepdims=True))
        alpha = jnp.exp2(m_prev - m_new)
        p = jnp.exp2(s - _rep(m_new, s.shape[-1]))
        l_scr[...] = alpha * l_scr[...] + jnp.sum(p, axis=-1, keepdims=True)
        m_scr[...] = m_new
        probs.append(p.astype(BF16))
        alphas.append(alpha)
    for p, alpha, v, (_, _, acc_scr) in zip(probs, alphas, values, scratch):
        acc_scr[...] = _rep(alpha, acc_scr.shape[-1]) * acc_scr[...] + _mm(p, v)


def _flash_init(scratch):
    for m_scr, l_scr, acc_scr in scratch:
        m_scr[...] = jnp.full(m_scr.shape, NEG, F32)
        l_scr[...] = jnp.zeros(l_scr.shape, F32)
        acc_scr[...] = jnp.zeros(acc_scr.shape, F32)


MLA_HEADS_PER_STEP = 8


def _mla_prompt_kernel(ii_ref, jj_ref, q_ref, k_ref, v_ref, o_ref, m_scr, l_scr, acc_scr):
    t = pl.program_id(1)
    i = ii_ref[t]
    j = jj_ref[t]
    tq, tk = q_ref.shape[1], k_ref.shape[1]
    scratch = [(m_scr.at[a], l_scr.at[a], acc_scr.at[a]) for a in range(MLA_HEADS_PER_STEP)]

    @pl.when(j == 0)
    def _():
        _flash_init(scratch)

    values = [v_ref[:, a * MLA_V:(a + 1) * MLA_V] for a in range(MLA_HEADS_PER_STEP)]

    @pl.when(j < i)
    def _():
        _flash_update([_mm_nt(q_ref[a], k_ref[a]) for a in range(MLA_HEADS_PER_STEP)], values, scratch)

    @pl.when(j == i)
    def _():
        visible = lax.broadcasted_iota(jnp.int32, (tq, tk), 1) <= lax.broadcasted_iota(jnp.int32, (tq, tk), 0)
        scores = [jnp.where(visible, _mm_nt(q_ref[a], k_ref[a]), NEG) for a in range(MLA_HEADS_PER_STEP)]
        _flash_update(scores, values, scratch)

    @pl.when(j == i)
    def _():
        for a in range(MLA_HEADS_PER_STEP):
            o_ref[:, a * MLA_V:(a + 1) * MLA_V] = acc_scr[a] / l_scr[a]


def _mla_prompt(qa, kn, va):
    rows = qa.shape[1]
    t = min(512, rows)
    assert rows % t == 0
    hp = MLA_HEADS_PER_STEP
    ii, jj = _tri_pairs(rows // t)
    grid_spec = pltpu.PrefetchScalarGridSpec(
        num_scalar_prefetch=2, grid=(MLA_HEADS // hp, ii.shape[0]),
        in_specs=[
            pl.BlockSpec((hp, t, MLA_QK_PAD), lambda h, s, ii, jj: (h, ii[s], 0)),
            pl.BlockSpec((hp, t, MLA_QK_PAD), lambda h, s, ii, jj: (h, jj[s], 0)),
            pl.BlockSpec((t, hp * MLA_V), lambda h, s, ii, jj: (jj[s], h)),
        ],
        out_specs=pl.BlockSpec((t, hp * MLA_V), lambda h, s, ii, jj: (ii[s], h)),
        scratch_shapes=[pltpu.VMEM((hp, t, LANE), F32), pltpu.VMEM((hp, t, LANE), F32),
                        pltpu.VMEM((hp, t, MLA_V), F32)],
    )
    return pl.pallas_call(
        _mla_prompt_kernel, grid_spec=grid_spec, out_shape=jax.ShapeDtypeStruct((rows, WIDTH_A), F32),
        compiler_params=_params(("arbitrary", "arbitrary")),
    )(ii, jj, qa, kn, va)


MOBA_KEY_BLOCKS = 2


def _moba_prompt_kernel(ii_ref, jj_ref, q_ref, b_ref, k_ref, v_ref, o_ref, m_scr, l_scr, acc_scr):
    t = pl.program_id(0)
    i = ii_ref[t]
    j = jj_ref[t]
    blk = MOBA_BLOCK
    nrow = MOBA_REP * blk
    scratch = [(m_scr.at[g], l_scr.at[g], acc_scr.at[g]) for g in range(MOBA_KV_HEADS)]

    @pl.when(j == 0)
    def _():
        _flash_init(scratch)

    tk = MOBA_KEY_BLOCKS * blk
    key_block = j * MOBA_KEY_BLOCKS + lax.broadcasted_iota(jnp.int32, (MOBA_KEY_BLOCKS, blk, LANE), 0).reshape(tk, LANE)
    onehot = jnp.where(lax.broadcasted_iota(jnp.int32, (tk, LANE), 1) == key_block, 1.0, 0.0).astype(BF16)
    own = i // MOBA_KEY_BLOCKS

    def raw_scores():
        out = []
        for g in range(MOBA_KV_HEADS):
            heads = slice(g * MOBA_REP, (g + 1) * MOBA_REP)
            q_aug = jnp.concatenate([q_ref[heads].reshape(nrow, MOBA_DIM), b_ref[heads].reshape(nrow, LANE)], axis=-1)
            k_aug = jnp.concatenate([k_ref[:, g * MOBA_DIM:(g + 1) * MOBA_DIM], onehot], axis=-1)
            out.append(_mm_nt(q_aug, k_aug))
        return out

    values = [v_ref[:, g * MOBA_DIM:(g + 1) * MOBA_DIM] for g in range(MOBA_KV_HEADS)]

    @pl.when(j < own)
    def _():
        _flash_update(raw_scores(), values, scratch)

    @pl.when(j == own)
    def _():
        qpos = i * blk + lax.broadcasted_iota(jnp.int32, (MOBA_REP, blk, tk), 1).reshape(nrow, tk)
        kpos = j * tk + lax.broadcasted_iota(jnp.int32, (nrow, tk), 1)
        _flash_update([jnp.where(kpos <= qpos, s, NEG) for s in raw_scores()], values, scratch)

    @pl.when(j == own)
    def _():
        for g in range(MOBA_KV_HEADS):
            o = acc_scr[g] / l_scr[g]
            for r in range(MOBA_REP):
                h = g * MOBA_REP + r
                o_ref[:, h * MOBA_DIM:(h + 1) * MOBA_DIM] = o[r * blk:(r + 1) * blk]


def _moba_prompt(qb16, bias, kb16, vb16):
    rows = qb16.shape[1]
    nb = rows // MOBA_BLOCK
    assert nb % MOBA_KEY_BLOCKS == 0
    pairs = [(i, j) for i in range(nb) for j in range(i // MOBA_KEY_BLOCKS + 1)]
    ii = jnp.asarray([p[0] for p in pairs], jnp.int32)
    jj = jnp.asarray([p[1] for p in pairs], jnp.int32)
    tk = MOBA_KEY_BLOCKS * MOBA_BLOCK
    nrow = MOBA_REP * MOBA_BLOCK
    kvw = MOBA_KV_HEADS * MOBA_DIM
    grid_spec = pltpu.PrefetchScalarGridSpec(
        num_scalar_prefetch=2, grid=(ii.shape[0],),
        in_specs=[
            pl.BlockSpec((MOBA_HEADS, MOBA_BLOCK, MOBA_DIM), lambda s, ii, jj: (0, ii[s], 0)),
            pl.BlockSpec((MOBA_HEADS, MOBA_BLOCK, LANE), lambda s, ii, jj: (0, ii[s], 0)),
            pl.BlockSpec((tk, kvw), lambda s, ii, jj: (jj[s], 0)),
            pl.BlockSpec((tk, kvw), lambda s, ii, jj: (jj[s], 0)),
        ],
        out_specs=pl.BlockSpec((MOBA_BLOCK, WIDTH_B), lambda s, ii, jj: (ii[s], 0)),
        scratch_shapes=[pltpu.VMEM((MOBA_KV_HEADS, nrow, LANE), F32), pltpu.VMEM((MOBA_KV_HEADS, nrow, LANE), F32),
                        pltpu.VMEM((MOBA_KV_HEADS, nrow, MOBA_DIM), F32)],
    )
    return pl.pallas_call(
        _moba_prompt_kernel, grid_spec=grid_spec, out_shape=jax.ShapeDtypeStruct((rows, WIDTH_B), F32),
        compiler_params=_params(("arbitrary",)),
    )(ii, jj, qb16, bias, kb16, vb16)


PAGES_PER_STEP = 16
COMBINE_UNROLL = 8


def _paged_copies(pt_ref, srcs, dsts, sem, b, j, slot):
    copies = []
    for p in range(PAGES_PER_STEP):
        page = pt_ref[b, j * PAGES_PER_STEP + p]
        for a, (src, dst) in enumerate(zip(srcs, dsts)):
            copies.append((pltpu.make_async_copy(src.at[page], dst(slot, p), sem.at[a, slot]), min(a, 1)))
    return copies


def _paged_pipeline(pt_ref, srcs, dsts, sem):
    b = pl.program_id(0)
    j = pl.program_id(1)
    nj = pl.num_programs(1)
    step = b * nj + j
    slot = step % 2

    @pl.when(step == 0)
    def _():
        for cp, prio in _paged_copies(pt_ref, srcs, dsts, sem, b, j, slot):
            cp.start(priority=prio)

    @pl.when(step + 1 < pl.num_programs(0) * nj)
    def _():
        last = j + 1 == nj
        nb_ = jnp.where(last, b + 1, b)
        nj_ = jnp.where(last, 0, j + 1)
        for cp, prio in _paged_copies(pt_ref, srcs, dsts, sem, nb_, nj_, 1 - slot):
            cp.start(priority=prio)

    for cp, _ in _paged_copies(pt_ref, srcs, dsts, sem, b, j, slot):
        cp.wait()
    return slot


def _mla_paged_step(slot, between, qbd_ref, qpe_ref, wukt_ref, wuv_ref, cnew_ref, knew_ref,
                    o_ref, lat_buf, kpe_buf, lhs_scr, m_scr, l_scr, ctx_scr):
    b = pl.program_id(0)
    j = pl.program_id(1)
    nj = pl.num_programs(1)
    nq = qbd_ref.shape[1]
    nhd = MLA_HEADS * MLA_NOPE
    scratch = [(m_scr, l_scr, ctx_scr)]

    @pl.when((b == 0) & (j == 0))
    def _():
        lhs_scr[:nhd, :] = wukt_ref[...]

    @pl.when(j == 0)
    def _():
        lhs_scr[nhd:, :] = _mm(qbd_ref[0], wukt_ref[...]).astype(BF16)
        _flash_init(scratch)

    def project(c):
        cb = c.astype(BF16)
        return cb, _mm_nt(lhs_scr[...], cb)

    def attend(cb, res, kpe_t, mask):
        n = cb.shape[0]
        kn = res[:nhd]
        n2 = jnp.sum((kn * kn).reshape(MLA_HEADS, MLA_NOPE, n), axis=1)
        pe2 = jnp.sum(kpe_t * kpe_t, axis=0, keepdims=True)
        spe = _mm(qpe_ref[0], kpe_t.astype(BF16))
        r = lax.rsqrt((n2 + pe2) * (1.0 / MLA_QK) + EPS)
        s = ((res[nhd:] + spe).reshape(nq // MLA_HEADS, MLA_HEADS, n) * r[None]).reshape(nq, n)
        if mask is not None:
            s = jnp.where(mask, s, NEG)
        _flash_update([s], [cb], scratch)

    cb, res = project(lat_buf[slot])
    between()
    attend(cb, res, jnp.concatenate([kpe_buf[slot, p] for p in range(PAGES_PER_STEP)], axis=-1), None)

    @pl.when(j == nj - 1)
    def _():
        nnew = cnew_ref.shape[1]
        qidx = lax.broadcasted_iota(jnp.int32, (nq // MLA_HEADS, MLA_HEADS, nnew), 0).reshape(nq, nnew)
        kidx = lax.broadcasted_iota(jnp.int32, (nq, nnew), 1)
        cb, res = project(cnew_ref[0])
        attend(cb, res, knew_ref[0], kidx <= qidx)
        ctx = (ctx_scr[...] / _rep(l_scr[...], MLA_KV_LORA)).astype(BF16)
        full = _mm(ctx, wuv_ref[...])
        hrow = lax.broadcasted_iota(jnp.int32, (nq // MLA_HEADS, MLA_HEADS, MLA_V), 1).reshape(nq, MLA_V)
        out = jnp.zeros((nq, MLA_V), F32)
        for h in range(MLA_HEADS):
            out = out + jnp.where(hrow == h, full[:, h * MLA_V:(h + 1) * MLA_V], 0.0)
        o_ref[0] = out


def _moba_paged_step(slot, q16_ref, qf_ref, knew_ref, vnew_ref, o_ref, k_buf, v_buf, m_s, l_s, g_s, o_s,
                     *, page_size, nb):
    j = pl.program_id(1)
    nj = pl.num_programs(1)
    nq = q16_ref.shape[1]
    blocks_per_step = PAGES_PER_STEP * page_size // MOBA_BLOCK

    def block(buf, u):
        base = u * MOBA_KV_HEADS * MOBA_BLOCK
        return jnp.concatenate(
            [buf[slot, pl.ds(base + g, MOBA_BLOCK, stride=MOBA_KV_HEADS), :] for g in range(MOBA_KV_HEADS)], axis=-1)

    q16 = q16_ref[0]
    qf = qf_ref[0]
    wide = (nq, LANE)
    k16 = []
    for u in range(blocks_per_step):
        kf = block(k_buf, u)
        kmean = jnp.mean(kf, axis=0, keepdims=True)
        g_s[j * blocks_per_step + u] = jnp.broadcast_to(jnp.sum(qf * kmean, axis=-1, keepdims=True), wide)
        k16.append(kf.astype(BF16))
    s_all = _mm_nt(q16, jnp.concatenate(k16, axis=0))
    probs = []
    for u in range(blocks_per_step):
        jb = j * blocks_per_step + u
        s = s_all[:, u * MOBA_BLOCK:(u + 1) * MOBA_BLOCK]
        m = jnp.max(s, axis=-1, keepdims=True)
        p = jnp.exp2(s - m)
        m_s[jb] = jnp.broadcast_to(m, wide)
        l_s[jb] = jnp.broadcast_to(jnp.sum(p, axis=-1, keepdims=True), wide)
        probs.append(p.astype(BF16))
    for u in range(blocks_per_step):
        o_s[j * blocks_per_step + u] = _mm(probs[u], block(v_buf, u).astype(BF16))

    @pl.when(j == nj - 1)
    def _():
        lane = lax.broadcasted_iota(jnp.int32, wide, 1)

        def gather(jb, carry):
            gm, mm, lm = carry
            hit = lane == jb
            return jnp.where(hit, g_s[jb], gm), jnp.where(hit, m_s[jb], mm), jnp.where(hit, l_s[jb], lm)
        gm, mm, lm = lax.fori_loop(
            0, nb, gather, (jnp.full(wide, -jnp.inf, F32), jnp.full(wide, NEG, F32), jnp.zeros(wide, F32)),
            unroll=COMBINE_UNROLL)

        def count(o, rank):
            other = g_s[o]
            ahead = (other > gm) | ((other == gm) & (o < lane))
            return rank + jnp.where(ahead, 1.0, 0.0)
        rank = lax.fori_loop(0, nb, count, jnp.zeros(wide, F32), unroll=COMBINE_UNROLL)
        keep = (rank < MOBA_TOPK) & (lane < nb)

        nnew = knew_ref.shape[1]
        qidx = lax.broadcasted_iota(jnp.int32, (nq // MOBA_HEADS, MOBA_HEADS, nnew), 0).reshape(nq, nnew)
        kidx = lax.broadcasted_iota(jnp.int32, (nq, nnew), 1)
        s_own = jnp.where(kidx <= qidx, _mm_nt(q16, knew_ref[0].astype(BF16)), NEG)
        m_own = jnp.max(s_own, axis=-1, keepdims=True)
        m_tot = jnp.maximum(jnp.max(jnp.where(keep, mm, NEG), axis=-1, keepdims=True), m_own)
        p_own = jnp.exp2(s_own - m_tot)
        wgt = jnp.exp2(jnp.where(keep, mm - m_tot, NEG))
        l_tot = jnp.sum(wgt * lm, axis=-1, keepdims=True) + jnp.sum(p_own, axis=-1, keepdims=True)
        acc0 = _mm(p_own.astype(BF16), vnew_ref[0].astype(BF16))

        def combine(jb, acc):
            w_col = jnp.sum(jnp.where(lane == jb, wgt, 0.0), axis=-1, keepdims=True)
            return acc + w_col * o_s[jb]
        acc = lax.fori_loop(0, nb, combine, acc0, unroll=COMBINE_UNROLL) / l_tot
        grow = lax.broadcasted_iota(jnp.int32, (nq // MOBA_HEADS, MOBA_HEADS, MOBA_DIM), 1).reshape(nq, MOBA_DIM)
        o_ref[0] = jnp.where(grow < MOBA_REP, acc[:, :MOBA_DIM], acc[:, MOBA_DIM:])


def _sample_attn_kernel(pt_ref, qbd_ref, qpe_ref, wukt_ref, wuv_ref, cnew_ref, cknew_ref, q16_ref, qf_ref,
                        knew_ref, vnew_ref, lat_hbm, kpe_hbm, k_hbm, v_hbm, oa_ref, ob_ref,
                        lat_buf, kpe_buf, k_buf, v_buf, sem, lhs_scr, m_scr, l_scr, ctx_scr, m_s, l_s, g_s, o_s,
                        *, page_size, nb):
    page_rows = MOBA_KV_HEADS * page_size
    window = lambda buf: (lambda s, p: buf.at[s, pl.ds(p * page_rows, page_rows)])
    slot = _paged_pipeline(
        pt_ref, (lat_hbm, kpe_hbm, k_hbm, v_hbm),
        (lambda s, p: lat_buf.at[s, pl.ds(p * page_size, page_size)], lambda s, p: kpe_buf.at[s, p],
         window(k_buf), window(v_buf)), sem)
    moba = functools.partial(_moba_paged_step, slot, q16_ref, qf_ref, knew_ref, vnew_ref, ob_ref, k_buf, v_buf,
                             m_s, l_s, g_s, o_s, page_size=page_size, nb=nb)
    _mla_paged_step(slot, lambda: None, qbd_ref, qpe_ref, wukt_ref, wuv_ref, cnew_ref, cknew_ref, oa_ref,
                    lat_buf, kpe_buf, lhs_scr, m_scr, l_scr, ctx_scr)
    moba()


def _sample_attention(page_table, qbd, qpe, wukt, wuv, cnew, cknew_t, q16, qf, knew, vnew,
                      cache_lat, cache_kpe_t, cache_k, cache_v):
    ns, n_pages = page_table.shape
    page_size = cache_lat.shape[1]
    past = n_pages * page_size
    assert n_pages % PAGES_PER_STEP == 0 and past % MOBA_BLOCK == 0
    assert (PAGES_PER_STEP * page_size) % MOBA_BLOCK == 0
    nb = past // MOBA_BLOCK
    assert nb <= LANE
    nq = qbd.shape[1]
    keys = PAGES_PER_STEP * page_size
    nhd = MLA_HEADS * MLA_NOPE
    kvw = MOBA_KV_HEADS * MOBA_DIM
    per_seq = lambda shape: pl.BlockSpec((1,) + shape, lambda b, j, pt: (b, 0, 0))
    hbm = pl.BlockSpec(memory_space=pl.ANY)
    grid_spec = pltpu.PrefetchScalarGridSpec(
        num_scalar_prefetch=1, grid=(ns, n_pages // PAGES_PER_STEP),
        in_specs=[
            per_seq((nq, nhd)), per_seq((nq, MLA_ROPE)),
            _const_spec((nhd, MLA_KV_LORA)), _const_spec((MLA_KV_LORA, MLA_HEADS * MLA_V)),
            per_seq(cnew.shape[1:]), per_seq(cknew_t.shape[1:]),
            per_seq((nq, kvw)), per_seq((nq, kvw)), per_seq(knew.shape[1:]), per_seq(vnew.shape[1:]),
            hbm, hbm, hbm, hbm,
        ],
        out_specs=(per_seq((nq, MLA_V)), per_seq((nq, MOBA_DIM))),
        scratch_shapes=[
            pltpu.VMEM((2, keys, MLA_KV_LORA), F32), pltpu.VMEM((2, PAGES_PER_STEP, MLA_ROPE, page_size), F32),
            pltpu.VMEM((2, keys * MOBA_KV_HEADS, MOBA_DIM), F32), pltpu.VMEM((2, keys * MOBA_KV_HEADS, MOBA_DIM), F32),
            pltpu.SemaphoreType.DMA((4, 2)),
            pltpu.VMEM((nhd + nq, MLA_KV_LORA), BF16),
            pltpu.VMEM((nq, LANE), F32), pltpu.VMEM((nq, LANE), F32), pltpu.VMEM((nq, MLA_KV_LORA), F32),
            pltpu.VMEM((nb, nq, LANE), F32), pltpu.VMEM((nb, nq, LANE), F32), pltpu.VMEM((nb, nq, LANE), F32),
            pltpu.VMEM((nb, nq, kvw), F32),
        ],
    )
    return pl.pallas_call(
        functools.partial(_sample_attn_kernel, page_size=page_size, nb=nb), grid_spec=grid_spec,
        out_shape=(jax.ShapeDtypeStruct((ns, nq, MLA_V), F32), jax.ShapeDtypeStruct((ns, nq, MOBA_DIM), F32)),
        compiler_params=_params(("arbitrary", "arbitrary")),
    )(page_table, qbd, qpe, wukt, wuv, cnew, cknew_t, q16, qf, knew, vnew, cache_lat, cache_kpe_t, cache_k, cache_v)


def _merge_kernel(x_ref, oa_ref, ob_ref, sga_ref, sgb_ref, w_ref, o_ref):
    mix = jnp.concatenate([(oa_ref[...] * sga_ref[...].astype(F32)).astype(BF16),
                           (ob_ref[...] * sgb_ref[...].astype(F32)).astype(BF16)], axis=-1)
    o_ref[...] = x_ref[...] + _mm(mix, w_ref[...])


def _even_merge(x, oa, ob, sga, sgb, w_out):
    rows = x.shape[0]
    tm = min(256, rows)
    assert rows % tm == 0
    row = lambda width: pl.BlockSpec((tm, width), lambda i: (i, 0))
    return pl.pallas_call(
        _merge_kernel, grid=(rows // tm,),
        in_specs=[row(D_MODEL), row(WIDTH_A), row(WIDTH_B), row(WIDTH_A), row(WIDTH_B),
                  _const_spec((WIDTH_A + WIDTH_B, D_MODEL))],
        out_specs=row(D_MODEL), out_shape=jax.ShapeDtypeStruct((rows, D_MODEL), F32),
        compiler_params=_params(("arbitrary",)),
    )(x, oa, ob, sga, sgb, w_out)


def _conv_kernel(x_ref, norm_ref, wb_ref, wc_ref, wh_ref, wg_ref, cw_ref, wo_ref, prev_ref,
                 o_ref, st_ref, xn_scr, acc_scr, carry_scr, *, step_major_seqs):
    i = pl.program_id(0)
    n = pl.program_id(1)
    tm = x_ref.shape[0]

    @pl.when(n == 0)
    def _():
        x = x_ref[...]
        xn_scr[...] = ((x * _rms(x, D_MODEL)) * norm_ref[...]).astype(BF16)
        acc_scr[...] = x

    xn = xn_scr[...]
    ch = _mm(xn, wc_ref[...]) * _mm(xn, wh_ref[...])
    cw = cw_ref[...]
    if step_major_seqs:
        ns = step_major_seqs
        full = jnp.concatenate([prev_ref[...], ch], axis=0)
        conv = cw[0:1] * full[0:tm] + cw[1:2] * full[ns:ns + tm] + cw[2:3] * full[2 * ns:2 * ns + tm]
        st_ref[...] = ch[tm - 2 * ns:]
    else:
        @pl.when(i == 0)
        def _():
            carry_scr[n] = jnp.zeros(carry_scr.shape[1:], F32)

        prev = carry_scr[n]
        rowi = lax.broadcasted_iota(jnp.int32, ch.shape, 0)
        m1 = jnp.where(rowi == 0, prev[SUBLANE - 1:SUBLANE], pltpu.roll(ch, 1, 0))
        m2 = jnp.where(rowi == 0, prev[SUBLANE - 2:SUBLANE - 1],
                       jnp.where(rowi == 1, prev[SUBLANE - 1:SUBLANE], pltpu.roll(ch, 2, 0)))
        conv = cw[0:1] * m2 + cw[1:2] * m1 + cw[2:3] * ch
        carry_scr[n] = ch[tm - SUBLANE:]
        st_ref[...] = ch[tm - SUBLANE:]
    y = (_mm(xn, wb_ref[...]) * conv * _silu(_mm(xn, wg_ref[...]))).astype(BF16)
    acc_scr[...] += _mm(y, wo_ref[...])

    @pl.when(n == pl.num_programs(1) - 1)
    def _():
        o_ref[...] = acc_scr[...]


def _conv_layer(x, prev, w, step_major_seqs):
    rows = x.shape[0]
    tm = rows if step_major_seqs else min(512, rows)
    tn = 512
    assert rows % tm == 0
    ncol = CONV_DIM // tn
    st_rows = prev.shape[0] if step_major_seqs else SUBLANE
    wcol = lambda k: pl.BlockSpec((D_MODEL, tn), lambda i, n, k=k: (0, n + k * ncol))
    return pl.pallas_call(
        functools.partial(_conv_kernel, step_major_seqs=step_major_seqs), grid=(rows // tm, ncol),
        in_specs=[
            pl.BlockSpec((tm, D_MODEL), lambda i, n: (i, 0)), _const_spec((1, D_MODEL)),
            wcol(0), wcol(1), wcol(2), wcol(3),
            pl.BlockSpec((CONV_WIDTH, tn), lambda i, n: (0, n)),
            pl.BlockSpec((tn, D_MODEL), lambda i, n: (n, 0)),
            pl.BlockSpec((prev.shape[0], tn), lambda i, n: (0, n)),
        ],
        out_specs=(pl.BlockSpec((tm, D_MODEL), lambda i, n: (i, 0)),
                   pl.BlockSpec((st_rows, tn), lambda i, n: (i, n))),
        out_shape=(jax.ShapeDtypeStruct((rows, D_MODEL), F32),
                   jax.ShapeDtypeStruct((rows // tm * st_rows, CONV_DIM), F32)),
        scratch_shapes=[pltpu.VMEM((tm, D_MODEL), BF16), pltpu.VMEM((tm, D_MODEL), F32),
                        pltpu.VMEM((ncol, SUBLANE, tn), F32)],
        compiler_params=_params(("arbitrary", "arbitrary")),
    )(x, w["norm"], w["w_in"], w["w_in"], w["w_in"], w["w_in"], w["conv_w"], w["w_out"], prev)


def _rot_cols(w):
    half = MLA_ROPE // 2
    return jnp.concatenate([-w[..., half:], w[..., :half]], axis=-1)


def _pad_cols(w, width):
    return jnp.pad(w, ((0, 0), (0, width - w.shape[-1])))


def _gain_pad(g):
    full = jnp.concatenate([g[:MLA_NOPE], g[MLA_NOPE:], g[MLA_NOPE:]])
    return jnp.pad(full, (0, MLA_QK_PAD - MLA_QK)).reshape(1, MLA_QK_PAD)


def _even_weights(norm, w_in, qa_norm, w_uq, kva_norm, w_uk, w_uv, q_norm, k_norm, moba_qn, moba_kn):
    o = np.cumsum((0, MLA_Q_LORA, MLA_KV_LORA, MLA_ROPE, WIDTH_B, MOBA_KV_HEADS * MOBA_DIM,
                   MOBA_KV_HEADS * MOBA_DIM, WIDTH_A, WIDTH_B))
    kpe_w = w_in[:, o[2]:o[3]]
    w_in2 = jnp.concatenate([
        w_in[:, o[0]:o[2]], _pad_cols(kpe_w, LANE), _pad_cols(_rot_cols(kpe_w), LANE), w_in[:, o[3]:],
    ], axis=-1).astype(BF16)
    uq = w_uq.reshape(MLA_Q_LORA, MLA_HEADS, MLA_QK)
    pe_w = uq[..., MLA_NOPE:]
    pad = jnp.zeros((MLA_Q_LORA, MLA_HEADS, LANE - MLA_ROPE), F32)
    uq2 = jnp.concatenate([uq[..., :MLA_NOPE], pe_w, pad, _rot_cols(pe_w), pad], axis=-1)
    return {
        "norm": norm.reshape(1, D_MODEL), "w_in": w_in2, "qa_norm": qa_norm.reshape(1, MLA_Q_LORA),
        "w_uq": uq2.reshape(MLA_Q_LORA, MLA_HEADS * UQ_HEAD_COLS).astype(BF16),
        "kva_norm": kva_norm.reshape(1, MLA_KV_LORA), "q_gain": _gain_pad(q_norm), "k_gain": _gain_pad(k_norm),
        "moba_qn": moba_qn.reshape(1, MOBA_DIM), "moba_kn": moba_kn.reshape(1, MOBA_DIM),
        "w_ukv": jnp.concatenate([w_uk, w_uv], axis=-1).astype(BF16),
    }


def _rope_tables(pos):
    half = MLA_ROPE // 2
    inv = ROPE_THETA ** (-jnp.arange(half, dtype=F32) / half)
    ang = pos.astype(F32)[:, None] * inv
    cos, sin = jnp.cos(ang), jnp.sin(ang)
    pad = jnp.zeros((pos.shape[0], LANE - MLA_ROPE), F32)
    return jnp.concatenate([cos, cos, pad], axis=-1), jnp.concatenate([sin, sin, pad], axis=-1)


def _block_diag_rows(x, nblk):
    eye = jnp.eye(nblk, dtype=x.dtype)
    out = x[..., :, None, :] * eye[:, :, None]
    return out.reshape(x.shape[:-1] + (nblk * x.shape[-1],))


def kernel(x_prompt, x_sample, cache_mla_latent, cache_mla_kpe, cache_moba_k, cache_moba_v, state_conv, page_table, norm_even, w_in_even, mla_qa_norm, mla_w_uq, mla_kva_norm, mla_w_uk, mla_w_uv, mla_q_norm, mla_k_norm, moba_q_norm, moba_k_norm, w_out_even, norm_odd, w_in_odd, conv_w, w_out_odd):
    assert x_prompt.shape[0] == 1 and norm_even.shape[0] == 1 and norm_odd.shape[0] == 1
    seq = x_prompt.shape[1]
    ns, steps = x_sample.shape[:2]
    n_pages = page_table.shape[1]
    page_size = cache_mla_latent.shape[2]
    past = n_pages * page_size
    assert seq % MOBA_BLOCK == 0 and steps <= SUBLANE
    kvw = MOBA_KV_HEADS * MOBA_DIM

    we = _even_weights(norm_even[0], w_in_even[0], mla_qa_norm[0], mla_w_uq[0], mla_kva_norm[0], mla_w_uk[0],
                       mla_w_uv[0], mla_q_norm[0], mla_k_norm[0], moba_q_norm[0], moba_k_norm[0])
    w_out_e = w_out_even[0].astype(BF16)
    wo = {"norm": norm_odd[0].reshape(1, D_MODEL), "w_in": w_in_odd[0].astype(BF16), "conv_w": conv_w[0],
          "w_out": w_out_odd[0].astype(BF16)}

    xp = x_prompt[0]
    cs, sn = _rope_tables(jnp.arange(seq))
    (qa, kn, va, lat_p, kpe_p, qb16, qbf, kb_p, vb_p, kb16, vb16, kmean, sga, sgb) = _even_project(xp, cs, sn, we, BF16)
    oa = _mla_prompt(qa, kn, va)
    bias = _moba_gate(qbf, kmean, seq // MOBA_BLOCK)
    ob = _moba_prompt(qb16, bias, kb16, vb16)
    y1p = _even_merge(xp, oa, ob, sga, sgb, w_out_e)
    y2p, st_p = _conv_layer(y1p, jnp.zeros((SUBLANE, CONV_DIM), F32), wo, 0)

    rows_s = steps * ns
    xs = x_sample.transpose(1, 0, 2).reshape(rows_s, D_MODEL)
    cs_s, sn_s = _rope_tables(past + jnp.repeat(jnp.arange(steps), ns))
    (qa_s, _, _, lat_s, kpe_s, qb16_s, qbf_s, kb_s, vb_s, _, _, _, sga_s, sgb_s) = _even_project(xs, cs_s, sn_s, we, F32)

    def per_seq(a):
        return a.reshape(steps, ns, a.shape[-1]).transpose(1, 0, 2)

    def pad_new(a):
        return jnp.pad(a, ((0, 0), (0, LANE - steps), (0, 0)))

    nq = steps * MLA_HEADS
    qk = qa_s.reshape(MLA_HEADS, steps, ns, MLA_QK_PAD).transpose(2, 1, 0, 3) * we["k_gain"][0]
    qbd = _block_diag_rows(qk[..., :MLA_NOPE], MLA_HEADS).reshape(ns, nq, MLA_HEADS * MLA_NOPE).astype(BF16)
    qpe = qk[..., MLA_NOPE:MLA_QK].reshape(ns, nq, MLA_ROPE).astype(BF16)

    def group_diag(a):
        g = a.reshape(ns, steps, MOBA_KV_HEADS, MOBA_REP, MOBA_DIM).transpose(0, 1, 3, 2, 4)
        return _block_diag_rows(g, MOBA_KV_HEADS).transpose(0, 1, 3, 2, 4).reshape(ns, nq, kvw)

    q16_s = group_diag(qb16_s.reshape(MOBA_HEADS, steps, ns, MOBA_DIM).transpose(2, 1, 0, 3))
    qf_s = group_diag(per_seq(qbf_s).reshape(ns, steps, MOBA_HEADS, MOBA_DIM))
    oa_s, ob_s = _sample_attention(
        page_table, qbd, qpe, mla_w_uk[0].T.astype(BF16), mla_w_uv[0].astype(BF16),
        pad_new(per_seq(lat_s)), pad_new(per_seq(kpe_s)).transpose(0, 2, 1),
        q16_s, qf_s, pad_new(per_seq(kb_s)), pad_new(per_seq(vb_s)),
        cache_mla_latent[0], cache_mla_kpe[0].transpose(0, 2, 1),
        cache_moba_k[0].reshape(-1, page_size * MOBA_KV_HEADS, MOBA_DIM),
        cache_moba_v[0].reshape(-1, page_size * MOBA_KV_HEADS, MOBA_DIM))

    def step_major(o):
        return o.reshape(ns, steps, -1).transpose(1, 0, 2).reshape(rows_s, -1)

    y1s = _even_merge(xs, step_major(oa_s), step_major(ob_s), sga_s, sgb_s, w_out_e)
    prev_s = state_conv[0].transpose(1, 0, 2).reshape((CONV_WIDTH - 1) * ns, CONV_DIM)
    y2s, st_s = _conv_layer(y1s, prev_s, wo, ns)

    return (
        y2p[None], per_seq(y2s),
        lat_p[None, None], kpe_p[None, None],
        kb_p.reshape(1, 1, seq, MOBA_KV_HEADS, MOBA_DIM), vb_p.reshape(1, 1, seq, MOBA_KV_HEADS, MOBA_DIM),
        st_p[st_p.shape[0] - (CONV_WIDTH - 1):][None, None],
        per_seq(lat_s)[None], per_seq(kpe_s)[None],
        per_seq(kb_s).reshape(1, ns, steps, MOBA_KV_HEADS, MOBA_DIM),
        per_seq(vb_s).reshape(1, ns, steps, MOBA_KV_HEADS, MOBA_DIM),
        st_s.reshape(CONV_WIDTH - 1, ns, CONV_DIM).transpose(1, 0, 2)[None],
    )
```
